```python
import math
import jax
import jax.numpy as jnp
from jax import lax
import numpy as np

D_MODEL = 2048
BATCH = 1
SEQ = 16384
DEPTH = 2

HEAD_DIM = 128
N_HEADS = D_MODEL // HEAD_DIM
BLOCK = 128
GRID_W = 64
ROW_WIN = 8
COL_WIN = 16
A_HEADS = N_HEADS // 2
B_HEADS = N_HEADS // 2
DIFF_QK_DIM = HEAD_DIM // 2
LAMBDA_INIT = 0.2
C_HEADS = N_HEADS // 2
C_PATTERNS = ((128, 1), (512, 4), (2048, 16))
C_GROUPS = len(C_PATTERNS)
D_HEADS = N_HEADS // 2
D_KV_HEADS = D_HEADS // 4
D_HALF_WINDOW = 128
ROPE_THETA = 500000.0
ROPE_FRACTION = 4
RMS_EPS = 1e-6
NEG_INF = -1e30

EVEN_IN_WIDTHS = (
    A_HEADS * HEAD_DIM, A_HEADS * HEAD_DIM, A_HEADS * HEAD_DIM, A_HEADS * HEAD_DIM,
    B_HEADS * 2 * DIFF_QK_DIM, B_HEADS * 2 * DIFF_QK_DIM, B_HEADS * HEAD_DIM, B_HEADS * HEAD_DIM,
)
EVEN_OUT = A_HEADS * HEAD_DIM + B_HEADS * HEAD_DIM
ODD_IN_WIDTHS = (
    C_GROUPS * C_HEADS * HEAD_DIM, C_GROUPS * C_HEADS * HEAD_DIM, C_GROUPS * C_HEADS * HEAD_DIM, C_HEADS * HEAD_DIM,
    D_HEADS * HEAD_DIM, D_KV_HEADS * HEAD_DIM, D_KV_HEADS * HEAD_DIM, D_HEADS * HEAD_DIM,
)
ODD_OUT = C_HEADS * HEAD_DIM + D_HEADS * HEAD_DIM

kernel_name = "hybrid_natten_diff_dilated_swa_encoder"


def rms_norm(x, g):
    xf = x.astype(jnp.float32)
    y = xf * lax.rsqrt(jnp.mean(xf * xf, axis=-1, keepdims=True) + RMS_EPS)
    return y.astype(x.dtype) * g


def _split(t, widths):
    out, start = [], 0
    for w in widths:
        out.append(t[..., start:start + w])
        start += w
    return out


def partial_rope(x, pos):
    dh = x.shape[-1]
    rd = dh // ROPE_FRACTION
    half = rd // 2
    inv = ROPE_THETA ** (-jnp.arange(half, dtype=jnp.float32) / half)
    ang = pos.astype(jnp.float32)[:, None] * inv[None, :]
    shape = (1, x.shape[1]) + (1,) * (x.ndim - 3) + (half,)
    cos = jnp.cos(ang).reshape(shape)
    sin = jnp.sin(ang).reshape(shape)
    xr = x[..., :rd].astype(jnp.float32)
    x1, x2 = xr[..., :half], xr[..., half:]
    rot = jnp.concatenate([x1 * cos - x2 * sin, x2 * cos + x1 * sin], axis=-1)
    return jnp.concatenate([rot.astype(x.dtype), x[..., rd:]], axis=-1)


def neighbourhood_attention(q, k, v, rpb):
    b, s, h, dh = q.shape
    rows = s // GRID_W
    wr = min(ROW_WIN, rows)
    r = jnp.arange(rows)
    row_start = jnp.clip(r - ROW_WIN // 2, 0, rows - wr)
    row_idx = row_start[:, None] + jnp.arange(wr)[None, :]
    qg = q.reshape(b, rows, GRID_W, h, dh)
    kg = k.reshape(b, rows, GRID_W, h, dh)[:, row_idx]
    vg = v.reshape(b, rows, GRID_W, h, dh)[:, row_idx]
    sc = jnp.einsum("brchd,brjkhd->brhcjk", qg, kg).astype(jnp.float32) * (dh ** -0.5)
    c = jnp.arange(GRID_W)
    col_start = jnp.clip(c - COL_WIN // 2, 0, GRID_W - COL_WIN)
    col_ok = (c[None, :] >= col_start[:, None]) & (c[None, :] < col_start[:, None] + COL_WIN)
    dr_i = row_idx - r[:, None] + ROW_WIN - 1
    dc_i = jnp.clip(c[None, :] - c[:, None] + COL_WIN - 1, 0, 2 * COL_WIN - 2)
    bias = rpb[:, dr_i][:, :, :, dc_i]
    bias = bias.transpose(1, 0, 3, 2, 4).astype(jnp.float32)
    sc = jnp.where(col_ok[:, None, :], sc + bias[None], NEG_INF)
    p = jax.nn.softmax(sc, axis=(-2, -1))
    o = jnp.einsum("brhcjk,brjkhd->brchd", p.astype(v.dtype), vg)
    return o.reshape(b, s, h, dh)


def diff_attention(q, k, v, lam):
    b, s, h, _, dq = q.shape
    dv = v.shape[-1]
    nb = s // BLOCK
    qb = jnp.moveaxis(q.reshape(b, nb, BLOCK, h, 2, dq), 1, 0)

    def one_block(qblk):
        sc = jnp.einsum("bqhcd,bkhcd->bchqk", qblk, k).astype(jnp.float32) * (dq ** -0.5)
        p = jax.nn.softmax(sc, axis=-1)
        a = p[:, 0] - lam * p[:, 1]
        return jnp.einsum("bhqk,bkhd->bqhd", a.astype(v.dtype), v)

    o = lax.map(one_block, qb)
    return jnp.moveaxis(o, 0, 1).reshape(b, s, h, dv)


def banded_attention(q, k, v, half):
    n, L, hk, g, dh = q.shape
    nb = -(-L // BLOCK)
    lp = nb * BLOCK
    kb = BLOCK + 2 * half
    qp = jnp.pad(q, ((0, 0), (0, lp - L), (0, 0), (0, 0), (0, 0)))
    pad_kv = ((0, 0), (half, lp - L + half), (0, 0), (0, 0))
    kp = jnp.pad(k, pad_kv)
    vp = jnp.pad(v, pad_kv)
    blk = jnp.arange(nb)[:, None] * BLOCK
    key_idx = blk + jnp.arange(kb)[None, :]
    k_blk = kp[:, key_idx]
    v_blk = vp[:, key_idx]
    q_blk = qp.reshape(n, nb, BLOCK, hk, g, dh)
    sc = jnp.einsum("nbqhgd,nbkhd->nbhgqk", q_blk, k_blk).astype(jnp.float32) * (dh ** -0.5)
    q_pos = blk + jnp.arange(BLOCK)[None, :]
    k_pos = key_idx - half
    valid = ((jnp.abs(q_pos[:, :, None] - k_pos[:, None, :]) <= half)
             & ((k_pos >= 0) & (k_pos < L))[:, None, :])
    sc = jnp.where(valid[None, :, None, None], sc, NEG_INF)
    lse = jax.nn.logsumexp(sc, axis=-1)
    p = jnp.exp(sc - lse[..., None])
    o = jnp.einsum("nbhgqk,nbkhd->nbqhgd", p.astype(v.dtype), v_blk)
    o = o.reshape(n, lp, hk, g, dh)[:, :L]
    lse = lse.transpose(0, 1, 4, 2, 3).reshape(n, lp, hk, g)[:, :L]
    return o, lse


def _fold(t, dil):
    b, s = t.shape[:2]
    rest = t.shape[2:]
    return t.reshape((b, s // dil, dil) + rest).swapaxes(1, 2).reshape((b * dil, s // dil) + rest)


def _unfold(t, b, dil):
    n, l = t.shape[:2]
    rest = t.shape[2:]
    return t.reshape((b, dil, l) + rest).swapaxes(1, 2).reshape((b, l * dil) + rest)


def dilated_mixture_attention(q, k, v):
    b = q.shape[0]
    outs, lses = [], []
    for gi, (window, dil) in enumerate(C_PATTERNS):
        half = (window // 2) // dil
        o, lse = banded_attention(_fold(q[:, :, gi], dil)[:, :, :, None],
                                  _fold(k[:, :, gi], dil), _fold(v[:, :, gi], dil), half)
        outs.append(_unfold(o[:, :, :, 0], b, dil))
        lses.append(_unfold(lse[:, :, :, 0], b, dil))
    w = jax.nn.softmax(jnp.stack(lses, axis=0), axis=0)
    o = jnp.stack(outs, axis=0)
    return jnp.sum(w[..., None].astype(o.dtype) * o, axis=0)


def even_layer(x, pre_g, w_in, rpb, lam_q1, lam_k1, lam_q2, lam_k2, subln_g, w_out, post_g):
    b, s, _ = x.shape
    pos = jnp.arange(s)
    h = rms_norm(x, pre_g)
    qa, ka, va, ga, qb, kb, vb, gb = _split(h @ w_in, EVEN_IN_WIDTHS)
    oa = neighbourhood_attention(qa.reshape(b, s, A_HEADS, HEAD_DIM), ka.reshape(b, s, A_HEADS, HEAD_DIM),
                                 va.reshape(b, s, A_HEADS, HEAD_DIM), rpb)
    qb = partial_rope(qb.reshape(b, s, B_HEADS, 2, DIFF_QK_DIM), pos)
    kb = partial_rope(kb.reshape(b, s, B_HEADS, 2, DIFF_QK_DIM), pos)
    lam = (jnp.exp(jnp.sum(lam_q1.astype(jnp.float32) * lam_k1.astype(jnp.float32)))
           - jnp.exp(jnp.sum(lam_q2.astype(jnp.float32) * lam_k2.astype(jnp.float32))) + LAMBDA_INIT)
    ob = diff_attention(qb, kb, vb.reshape(b, s, B_HEADS, HEAD_DIM), lam)
    ob = rms_norm(ob, subln_g) * (1.0 - LAMBDA_INIT)
    mix = jnp.concatenate([oa.reshape(b, s, -1) * jax.nn.silu(ga),
                           ob.reshape(b, s, -1) * jax.nn.silu(gb)], axis=-1)
    return x + rms_norm(mix @ w_out, post_g)


def odd_layer(x, pre_g, w_in, sink, w_out, post_g):
    b, s, _ = x.shape
    pos = jnp.arange(s)
    h = rms_norm(x, pre_g)
    qc, kc, vc, gc, qd, kd, vd, gd = _split(h @ w_in, ODD_IN_WIDTHS)
    qc = partial_rope(qc.reshape(b, s, C_GROUPS, C_HEADS, HEAD_DIM), pos)
    kc = partial_rope(kc.reshape(b, s, C_GROUPS, C_HEADS, HEAD_DIM), pos)
    oc = dilated_mixture_attention(qc, kc, vc.reshape(b, s, C_GROUPS, C_HEADS, HEAD_DIM))
    grp = D_HEADS // D_KV_HEADS
    qd = partial_rope(qd.reshape(b, s, D_HEADS, HEAD_DIM), pos).reshape(b, s, D_KV_HEADS, grp, HEAD_DIM)
    kd = partial_rope(kd.reshape(b, s, D_KV_HEADS, HEAD_DIM), pos)
    od, lse = banded_attention(qd, kd, vd.reshape(b, s, D_KV_HEADS, HEAD_DIM), D_HALF_WINDOW)
    sink_f = sink.astype(jnp.float32).reshape(D_KV_HEADS, grp)
    od = od * jnp.exp(lse - jnp.logaddexp(lse, sink_f))[..., None].astype(od.dtype)
    mix = jnp.concatenate([oc.reshape(b, s, -1) * jax.nn.silu(gc),
                           od.reshape(b, s, -1) * jax.nn.silu(gd)], axis=-1)
    return x + rms_norm(mix @ w_out, post_g)


def setup_inputs(seed: int = 0) -> dict:
    key = jax.random.key(seed)
    ks = jax.random.split(key, 16)
    d = D_MODEL

    def nrm(k, shape, scale):
        return jax.random.normal(k, shape, jnp.float32) * scale

    even_in = sum(EVEN_IN_WIDTHS)
    odd_in = sum(ODD_IN_WIDTHS)
    return {
        "x": nrm(ks[0], (BATCH, SEQ, D_MODEL), 1.0),
        "l0_pre_g": 1.0 + nrm(ks[1], (d,), 0.05),
        "l0_w_in": nrm(ks[2], (d, even_in), d ** -0.5),
        "l0_rpb": nrm(ks[3], (A_HEADS, 2 * ROW_WIN - 1, 2 * COL_WIN - 1), 0.5),
        "l0_lam_q1": nrm(ks[4], (DIFF_QK_DIM,), 0.1),
        "l0_lam_k1": nrm(ks[5], (DIFF_QK_DIM,), 0.1),
        "l0_lam_q2": nrm(ks[6], (DIFF_QK_DIM,), 0.1),
        "l0_lam_k2": nrm(ks[7], (DIFF_QK_DIM,), 0.1),
        "l0_subln_g": 1.0 + nrm(ks[8], (HEAD_DIM,), 0.05),
        "l0_w_out": nrm(ks[9], (EVEN_OUT, d), EVEN_OUT ** -0.5),
        "l0_post_g": 1.0 + nrm(ks[10], (d,), 0.05),
        "l1_pre_g": 1.0 + nrm(ks[11], (d,), 0.05),
        "l1_w_in": nrm(ks[12], (d, odd_in), d ** -0.5),
        "l1_sink": nrm(ks[13], (D_HEADS,), 1.0),
        "l1_w_out": nrm(ks[14], (ODD_OUT, d), ODD_OUT ** -0.5),
        "l1_post_g": 1.0 + nrm(ks[15], (d,), 0.05),
    }


def reference(x, l0_pre_g, l0_w_in, l0_rpb, l0_lam_q1, l0_lam_k1, l0_lam_q2, l0_lam_k2,
              l0_subln_g, l0_w_out, l0_post_g, l1_pre_g, l1_w_in, l1_sink, l1_w_out, l1_post_g):
    layer_params = (
        (l0_pre_g, l0_w_in, l0_rpb, l0_lam_q1, l0_lam_k1, l0_lam_q2, l0_lam_k2, l0_subln_g, l0_w_out, l0_post_g),
        (l1_pre_g, l1_w_in, l1_sink, l1_w_out, l1_post_g),
    )
    for layer in range(DEPTH):
        if layer % 2 == 0:
            x = even_layer(x, *layer_params[layer])
        else:
            x = odd_layer(x, *layer_params[layer])
    return x
```

```python
import functools
import math

import numpy as np
import jax
import jax.numpy as jnp
from jax import lax
from jax.experimental import pallas as pl
from jax.experimental.pallas import tpu as pltpu

F32 = jnp.float32
BF16 = jnp.bfloat16

D_MODEL = 2048
HEAD_DIM = 128
N_HALF_HEADS = 8
SEC = N_HALF_HEADS * HEAD_DIM
GRID_W = 64
ROW_WIN = 8
COL_WIN = 16
DIFF_QK_DIM = 64
LAMBDA_INIT = 0.2
C_PATTERNS = ((128, 1), (512, 4), (2048, 16))
D_KV_HEADS = 2
D_HALF_WINDOW = 128
ROPE_THETA = 500000.0
ROPE_FRACTION = 4
RMS_EPS = 1e-6
NEG_INF = -1e30

LANES = 128
VMEM_LIMIT_BYTES = 56 * 1024 * 1024

PLAIN, ROPE = 0, 1
CHUNKS = SEC // LANES
ALL_PLAIN = (PLAIN,) * CHUNKS
ALL_ROPE = (ROPE,) * CHUNKS
NORM_ROWS = 256

NT_DIMS = (((1,), (1,)), ((), ()))


def _silu(g):
    return g / (1.0 + jnp.exp(-g))


def _inproj_kernel(sections, rope_half, x_ref, g_ref, w_ref, cos_ref, sa_ref, sb_ref,
                   o_ref, h_ref):
    j = pl.program_id(1)

    @pl.when(j == 0)
    def _():
        tm = x_ref.shape[0]
        rc = min(NORM_ROWS, tm)
        for r in range(tm // rc):
            x = x_ref[r * rc:(r + 1) * rc, :]
            ms = jnp.mean(x * x, axis=-1, keepdims=True)
            h_ref[r * rc:(r + 1) * rc, :] = (x * lax.rsqrt(ms + RMS_EPS) * g_ref[...]).astype(BF16)

    acc = jnp.dot(h_ref[...], w_ref[...], preferred_element_type=F32)

    groups = {}
    for idx, spec in enumerate(sections):
        groups.setdefault(spec, []).append(idx)

    for (modes, scale), idxs in groups.items():
        cond = functools.reduce(jnp.logical_or, [j == i for i in idxs])

        @pl.when(cond)
        def _(modes=modes, scale=scale):
            for c in range(SEC // LANES):
                a = acc[:, c * LANES:(c + 1) * LANES]
                if modes[c] == ROPE:
                    a = (a * cos_ref[...]
                         + pltpu.roll(a, LANES - rope_half, 1) * sa_ref[...]
                         + pltpu.roll(a, rope_half, 1) * sb_ref[...])
                if scale != 1.0:
                    a = a * scale
                o_ref[:, c * LANES:(c + 1) * LANES] = a.astype(BF16)


def _inproj(x, pre_g, w, tables, sections, rope_half, tm):
    s = x.shape[0]
    nsec = len(sections)
    cos_t, sa_t, sb_t = tables
    tab_spec = pl.BlockSpec((tm, LANES), lambda i, j: (i, 0))
    return pl.pallas_call(
        functools.partial(_inproj_kernel, tuple(sections), rope_half),
        grid=(s // tm, nsec),
        in_specs=[
            pl.BlockSpec((tm, D_MODEL), lambda i, j: (i, 0)),
            pl.BlockSpec((1, D_MODEL), lambda i, j: (0, 0)),
            pl.BlockSpec((D_MODEL, SEC), lambda i, j: (0, j)),
            tab_spec, tab_spec, tab_spec,
        ],
        out_specs=pl.BlockSpec((None, tm, SEC), lambda i, j: (j, i, 0)),
        out_shape=jax.ShapeDtypeStruct((nsec, s, SEC), BF16),
        scratch_shapes=[pltpu.VMEM((tm, D_MODEL), BF16)],
        compiler_params=pltpu.CompilerParams(
            dimension_semantics=("arbitrary", "arbitrary"),
            vmem_limit_bytes=VMEM_LIMIT_BYTES),
        name="inproj",
    )(x, pre_g.reshape(1, D_MODEL), w, cos_t, sa_t, sb_t)


def _rope_tables(s, comp_dim):
    rd = comp_dim // ROPE_FRACTION
    half = rd // 2
    inv = ROPE_THETA ** (-jnp.arange(half, dtype=F32) / half)
    ang = jnp.arange(s).astype(F32)[:, None] * inv[None, :]
    cos, sin = jnp.cos(ang), jnp.sin(ang)
    ones = jnp.ones((s, comp_dim - rd), F32)
    zeros_h = jnp.zeros((s, half), F32)
    zeros_r = jnp.zeros((s, comp_dim - rd), F32)
    reps = LANES // comp_dim
    cos_t = jnp.tile(jnp.concatenate([cos, cos, ones], axis=1), (1, reps))
    sa_t = jnp.tile(jnp.concatenate([-sin, zeros_h, zeros_r], axis=1), (1, reps))
    sb_t = jnp.tile(jnp.concatenate([zeros_h, sin, zeros_r], axis=1), (1, reps))
    return (cos_t, sa_t, sb_t), half


NA_QROWS = 4
NA_KROWS = 12
NA_TQ = NA_QROWS * GRID_W
NA_KBLK = 3


def _natten_kernel(q_ref, k0_ref, k1_ref, k2_ref, v0_ref, v1_ref, v2_ref, g_ref, bias_ref,
                   o_ref):
    k_refs = (k0_ref, k1_ref, k2_ref)
    v_refs = (v0_ref, v1_ref, v2_ref)
    for h in range(N_HALF_HEADS):
        cols = slice(h * HEAD_DIM, (h + 1) * HEAD_DIM)
        q = q_ref[:, cols]
        scores = []
        for t in range(NA_KBLK):
            st = lax.dot_general(q, k_refs[t][:, cols], NT_DIMS, preferred_element_type=F32)
            scores.append(st + bias_ref[h, :, t * NA_TQ:(t + 1) * NA_TQ])
        m = functools.reduce(jnp.maximum, [jnp.max(st, axis=-1, keepdims=True) for st in scores])
        ps = [jnp.exp(st - m) for st in scores]
        l = functools.reduce(jnp.add, [jnp.sum(p, axis=-1, keepdims=True) for p in ps])
        o = functools.reduce(jnp.add, [
            jnp.dot(ps[t].astype(BF16), v_refs[t][:, cols], preferred_element_type=F32)
            for t in range(NA_KBLK)])
        o = o / l
        o_ref[:, cols] = (o * _silu(g_ref[:, cols].astype(F32))).astype(BF16)


def _natten_bias_table(rpb, rows):
    wr = min(ROW_WIN, rows)
    nblk = rows // NA_QROWS
    tabs = []
    for blk in (0, 1, nblk - 1):
        r0 = blk * NA_QROWS
        ws = min(max(blk - 1, 0), nblk - NA_KBLK) * NA_QROWS
        r = r0 + np.arange(NA_QROWS)[:, None, None, None]
        qc = np.arange(GRID_W)[None, :, None, None]
        kr = ws + np.arange(NA_KROWS)[None, None, :, None]
        kc = np.arange(GRID_W)[None, None, None, :]
        rs = np.clip(r - ROW_WIN // 2, 0, rows - wr)
        row_ok = (kr >= rs) & (kr < rs + wr)
        dr = np.clip(kr - r + ROW_WIN - 1, 0, 2 * ROW_WIN - 2)
        cs = np.clip(qc - COL_WIN // 2, 0, GRID_W - COL_WIN)
        col_ok = (kc >= cs) & (kc < cs + COL_WIN)
        dc = np.clip(kc - qc + COL_WIN - 1, 0, 2 * COL_WIN - 2)
        shape = (NA_QROWS, GRID_W, NA_KROWS, GRID_W)
        valid = np.broadcast_to(row_ok & col_ok, shape).reshape(NA_TQ, NA_KBLK * NA_TQ)
        flat = np.broadcast_to(dr * (2 * COL_WIN - 1) + dc, shape).reshape(NA_TQ, NA_KBLK * NA_TQ)
        vals = jnp.take(rpb.reshape(rpb.shape[0], -1), jnp.asarray(flat), axis=1)
        tabs.append(jnp.where(jnp.asarray(valid)[None], vals, NEG_INF))
    return jnp.stack(tabs, axis=0).astype(F32)


def _natten(proj, rpb):
    s = proj.shape[1]
    rows = s // GRID_W
    nblk = rows // NA_QROWS
    bias = _natten_bias_table(rpb, rows)

    def kv_spec(sec, t):
        return pl.BlockSpec(
            (None, NA_TQ, SEC),
            lambda b: (sec, jnp.clip(b - 1, 0, nblk - NA_KBLK) + t, 0))

    def bias_idx(b):
        return (jnp.where(b == 0, 0, jnp.where(b == nblk - 1, 2, 1)), 0, 0, 0)

    return pl.pallas_call(
        _natten_kernel,
        grid=(nblk,),
        in_specs=[
            pl.BlockSpec((None, NA_TQ, SEC), lambda b: (0, b, 0)),
            kv_spec(1, 0), kv_spec(1, 1), kv_spec(1, 2),
            kv_spec(2, 0), kv_spec(2, 1), kv_spec(2, 2),
            pl.BlockSpec((None, NA_TQ, SEC), lambda b: (3, b, 0)),
            pl.BlockSpec((None, N_HALF_HEADS, NA_TQ, NA_KBLK * NA_TQ), bias_idx),
        ],
        out_specs=pl.BlockSpec((NA_TQ, SEC), lambda b: (b, 0)),
        out_shape=jax.ShapeDtypeStruct((s, SEC), BF16),
        compiler_params=pltpu.CompilerParams(
            dimension_semantics=("arbitrary",), vmem_limit_bytes=VMEM_LIMIT_BYTES),
        name="natten",
    )(proj, proj, proj, proj, proj, proj, proj, proj, bias)


def _diff_kernel(tq, tk, q_ref, k_ref, vt_ref, g_ref, lam_ref, subg_ref, o_ref,
                 qbd_ref, m_ref, l_ref, acc_ref):
    s = k_ref.shape[0]
    q = q_ref[...]
    lane = lax.broadcasted_iota(jnp.int32, (tq, HEAD_DIM), 1)
    zero = jnp.zeros_like(q)
    qbd_ref[0:tq, :] = jnp.where(lane < DIFF_QK_DIM, q, zero)
    qbd_ref[tq:2 * tq, :] = jnp.where(lane >= DIFF_QK_DIM, q, zero)
    m_ref[...] = jnp.full(m_ref.shape, NEG_INF, F32)
    l_ref[...] = jnp.zeros(l_ref.shape, F32)
    acc_ref[...] = jnp.zeros(acc_ref.shape, F32)

    def body(kk, carry):
        start = pl.multiple_of(kk * tk, tk)
        k = k_ref[pl.ds(start, tk), :]
        st = lax.dot_general(k, qbd_ref[...], NT_DIMS, preferred_element_type=F32)
        m_prev = m_ref[...]
        m_new = jnp.maximum(m_prev, jnp.max(st, axis=0, keepdims=True))
        alpha = jnp.exp(m_prev - m_new)
        p = jnp.exp(st - m_new)
        l_ref[...] = alpha * l_ref[...] + jnp.sum(p, axis=0, keepdims=True)
        vt = vt_ref[:, pl.ds(start, tk)]
        acc_ref[...] = alpha * acc_ref[...] + jnp.dot(
            vt, p.astype(BF16), preferred_element_type=F32)
        m_ref[...] = m_new
        return carry

    lax.fori_loop(0, s // tk, body, 0)

    lam_v = lam_ref[...]
    lam = (jnp.exp(jnp.sum(lam_v[0:1] * lam_v[1:2], axis=-1, keepdims=True))
           - jnp.exp(jnp.sum(lam_v[2:3] * lam_v[3:4], axis=-1, keepdims=True))
           + LAMBDA_INIT)
    on = acc_ref[...] / l_ref[...]
    ot = on[:, 0:tq] - lam * on[:, tq:2 * tq]
    ms = jnp.mean(ot * ot, axis=0, keepdims=True)
    ot = ot * lax.rsqrt(ms + RMS_EPS) * subg_ref[...] * (1.0 - LAMBDA_INIT)
    o = ot.T
    o_ref[...] = (o * _silu(g_ref[...].astype(F32))).astype(BF16)


def _diff_attention(proj, vt, lam_vecs, subln_g, tq, tk):
    s = proj.shape[1]
    return pl.pallas_call(
        functools.partial(_diff_kernel, tq, tk),
        grid=(N_HALF_HEADS, s // tq),
        in_specs=[
            pl.BlockSpec((None, tq, HEAD_DIM), lambda h, i: (4, i, h)),
            pl.BlockSpec((None, s, HEAD_DIM), lambda h, i: (5, 0, h)),
            pl.BlockSpec((None, HEAD_DIM, s), lambda h, i: (h, 0, 0)),
            pl.BlockSpec((None, tq, HEAD_DIM), lambda h, i: (7, i, h)),
            pl.BlockSpec((4, DIFF_QK_DIM), lambda h, i: (0, 0)),
            pl.BlockSpec((HEAD_DIM, 1), lambda h, i: (0, 0)),
        ],
        out_specs=pl.BlockSpec((tq, HEAD_DIM), lambda h, i: (i, h)),
        out_shape=jax.ShapeDtypeStruct((s, SEC), BF16),
        scratch_shapes=[
            pltpu.VMEM((2 * tq, HEAD_DIM), BF16),
            pltpu.VMEM((1, 2 * tq), F32),
            pltpu.VMEM((1, 2 * tq), F32),
            pltpu.VMEM((HEAD_DIM, 2 * tq), F32),
        ],
        compiler_params=pltpu.CompilerParams(
            dimension_semantics=("arbitrary", "arbitrary"),
            vmem_limit_bytes=VMEM_LIMIT_BYTES),
        name="diff_attn",
    )(proj, proj, vt, proj, lam_vecs, subln_g.reshape(HEAD_DIM, 1))


def _banded_kernel(tq, half, kcols, vcols, with_sink, *refs):
    if with_sink:
        (sink_ref, q_ref, km_ref, kp_ref, kn_ref, vm_ref, vp_ref, vn_ref, g_ref, o_ref) = refs
    else:
        (q_ref, km_ref, kp_ref, kn_ref, vm_ref, vp_ref, vn_ref, o_ref, lse_ref) = refs
    lb = pl.program_id(1)
    nblk = pl.num_programs(1)

    qi = lax.broadcasted_iota(jnp.int32, (tq, tq), 0)
    kj = lax.broadcasted_iota(jnp.int32, (tq, tq), 1)
    ok_main = jnp.abs(qi - kj) <= half
    qh = lax.broadcasted_iota(jnp.int32, (tq, 2 * half), 0)
    jh = lax.broadcasted_iota(jnp.int32, (tq, 2 * half), 1)
    prev_lo = jnp.where(lb > 0, 0, half)
    next_hi = jnp.where(lb < nblk - 1, 2 * half, half)
    ok_prev = (jh >= prev_lo) & (jh < half) & (qh <= jh)
    ok_next = (jh >= half) & (jh < next_hi) & (tq + (jh - half) - qh <= half)
    ok_halo = ok_prev | ok_next

    if not with_sink:
        lse_ref[...] = jnp.zeros(lse_ref.shape, F32)

    for h in range(N_HALF_HEADS):
        cols = slice(h * HEAD_DIM, (h + 1) * HEAD_DIM)
        kc = slice(kcols[h], kcols[h] + HEAD_DIM)
        vc = slice(vcols[h], vcols[h] + HEAD_DIM)
        q = q_ref[:, cols]
        k_halo = jnp.concatenate([kp_ref[:, kc], kn_ref[:, kc]], axis=0)
        v_halo = jnp.concatenate([vp_ref[:, vc], vn_ref[:, vc]], axis=0)
        s_m = lax.dot_general(q, km_ref[:, kc], NT_DIMS, preferred_element_type=F32)
        s_h = lax.dot_general(q, k_halo, NT_DIMS, preferred_element_type=F32)
        s_m = jnp.where(ok_main, s_m, NEG_INF)
        s_h = jnp.where(ok_halo, s_h, NEG_INF)
        m = jnp.maximum(jnp.max(s_m, axis=-1, keepdims=True), jnp.max(s_h, axis=-1, keepdims=True))
        p_m = jnp.exp(s_m - m)
        p_h = jnp.exp(s_h - m)
        l = jnp.sum(p_m, axis=-1, keepdims=True) + jnp.sum(p_h, axis=-1, keepdims=True)
        acc = (jnp.dot(p_m.astype(BF16), vm_ref[:, vc], preferred_element_type=F32)
               + jnp.dot(p_h.astype(BF16), v_halo, preferred_element_type=F32))
        if with_sink:
            o = acc / (l + jnp.exp(sink_ref[h] - m))
            o_ref[:, cols] = (o * _silu(g_ref[:, cols].astype(F32))).astype(BF16)
        else:
            o_ref[:, cols] = acc / l
            lse_ref[:, h:h + 1] = m + jnp.log(l)


def _banded_specs(tq, half, dil, q_sec, k_sec, v_sec, nblk):
    r = tq // half

    def main(sec):
        return pl.BlockSpec((None, tq, SEC), lambda c, lb, *_: (sec, lb, c))

    def prev(sec):
        return pl.BlockSpec((None, half, SEC),
                            lambda c, lb, *_: (sec, jnp.maximum(lb * r - 1, 0), c))

    def nxt(sec):
        return pl.BlockSpec((None, half, SEC),
                            lambda c, lb, *_: (sec, jnp.minimum((lb + 1) * r, nblk * r - 1), c))

    return [main(q_sec), main(k_sec), prev(k_sec), nxt(k_sec),
            main(v_sec), prev(v_sec), nxt(v_sec)]


def _dilated_group(proj, group, dil, tq):
    nsec, s, _ = proj.shape
    half = (C_PATTERNS[group][0] // 2) // dil
    length = s // dil
    tq = min(tq, length)
    nblk = length // tq
    folded = proj.reshape(nsec, length, dil * SEC)
    cols = tuple(h * HEAD_DIM for h in range(N_HALF_HEADS))
    specs = _banded_specs(tq, half, dil, group, 3 + group, 6 + group, nblk)
    o, lse = pl.pallas_call(
        functools.partial(_banded_kernel, tq, half, cols, cols, False),
        grid=(dil, nblk),
        in_specs=specs,
        out_specs=[pl.BlockSpec((tq, SEC), lambda c, lb: (lb, c)),
                   pl.BlockSpec((tq, LANES), lambda c, lb: (lb, c))],
        out_shape=[jax.ShapeDtypeStruct((length, dil * SEC), F32),
                   jax.ShapeDtypeStruct((length, dil * LANES), F32)],
        compiler_params=pltpu.CompilerParams(
            dimension_semantics=("arbitrary", "arbitrary"),
            vmem_limit_bytes=VMEM_LIMIT_BYTES),
        name=f"dilated_g{group}",
    )(*([folded] * 7))
    return o.reshape(s, SEC), lse.reshape(s, LANES)


def _windowed_sink(proj, sink, tq):
    s = proj.shape[1]
    half = D_HALF_WINDOW
    nblk = s // tq
    grp = N_HALF_HEADS // D_KV_HEADS
    kcols = tuple((h // grp) * HEAD_DIM for h in range(N_HALF_HEADS))
    vcols = tuple(D_KV_HEADS * HEAD_DIM + (h // grp) * HEAD_DIM for h in range(N_HALF_HEADS))
    specs = _banded_specs(tq, half, 1, 10, 11, 11, nblk)
    specs.append(pl.BlockSpec((None, tq, SEC), lambda c, lb, *_: (12, lb, 0)))
    grid_spec = pltpu.PrefetchScalarGridSpec(
        num_scalar_prefetch=1,
        grid=(1, nblk),
        in_specs=specs,
        out_specs=pl.BlockSpec((tq, SEC), lambda c, lb, *_: (lb, 0)),
    )
    return pl.pallas_call(
        functools.partial(_banded_kernel, tq, half, kcols, vcols, True),
        grid_spec=grid_spec,
        out_shape=jax.ShapeDtypeStruct((s, SEC), BF16),
        compiler_params=pltpu.CompilerParams(
            dimension_semantics=("arbitrary", "arbitrary"),
            vmem_limit_bytes=VMEM_LIMIT_BYTES),
        name="windowed_sink",
    )(sink.astype(F32), *([proj] * 8))


def _merge_kernel(o0_ref, o1_ref, o2_ref, l0_ref, l1_ref, l2_ref, g_ref, out_ref):
    for h in range(N_HALF_HEADS):
        cols = slice(h * HEAD_DIM, (h + 1) * HEAD_DIM)
        ls = [r[:, h:h + 1] for r in (l0_ref, l1_ref, l2_ref)]
        m = jnp.maximum(jnp.maximum(ls[0], ls[1]), ls[2])
        es = [jnp.exp(x - m) for x in ls]
        den = es[0] + es[1] + es[2]
        o = ((es[0] / den) * o0_ref[:, cols] + (es[1] / den) * o1_ref[:, cols]
             + (es[2] / den) * o2_ref[:, cols])
        out_ref[:, cols] = (o * _silu(g_ref[:, cols].astype(F32))).astype(BF16)


def _merge_groups(outs, lses, proj, tm):
    s = proj.shape[1]
    o_spec = pl.BlockSpec((tm, SEC), lambda i: (i, 0))
    l_spec = pl.BlockSpec((tm, LANES), lambda i: (i, 0))
    return pl.pallas_call(
        _merge_kernel,
        grid=(s // tm,),
        in_specs=[o_spec, o_spec, o_spec, l_spec, l_spec, l_spec,
                  pl.BlockSpec((None, tm, SEC), lambda i: (9, i, 0))],
        out_specs=pl.BlockSpec((tm, SEC), lambda i: (i, 0)),
        out_shape=jax.ShapeDtypeStruct((s, SEC), BF16),
        compiler_params=pltpu.CompilerParams(
            dimension_semantics=("arbitrary",), vmem_limit_bytes=VMEM_LIMIT_BYTES),
        name="merge_groups",
    )(*outs, *lses, proj)


def _outproj_kernel(a_ref, b_ref, w_ref, g_ref, x_ref, o_ref):
    y = (jnp.dot(a_ref[...], w_ref[0:SEC, :], preferred_element_type=F32)
         + jnp.dot(b_ref[...], w_ref[SEC:2 * SEC, :], preferred_element_type=F32))
    ms = jnp.mean(y * y, axis=-1, keepdims=True)
    o_ref[...] = x_ref[...] + y * lax.rsqrt(ms + RMS_EPS) * g_ref[...]


def _outproj(mix_a, mix_b, w_out, post_g, x, tm):
    s = x.shape[0]
    half_spec = pl.BlockSpec((tm, SEC), lambda i: (i, 0))
    return pl.pallas_call(
        _outproj_kernel,
        grid=(s // tm,),
        in_specs=[half_spec, half_spec,
                  pl.BlockSpec((2 * SEC, D_MODEL), lambda i: (0, 0)),
                  pl.BlockSpec((1, D_MODEL), lambda i: (0, 0)),
                  pl.BlockSpec((tm, D_MODEL), lambda i: (i, 0))],
        out_specs=pl.BlockSpec((tm, D_MODEL), lambda i: (i, 0)),
        out_shape=jax.ShapeDtypeStruct((s, D_MODEL), F32),
        compiler_params=pltpu.CompilerParams(
            dimension_semantics=("arbitrary",), vmem_limit_bytes=VMEM_LIMIT_BYTES),
        name="outproj",
    )(mix_a, mix_b, w_out, post_g.reshape(1, D_MODEL), x)


def _tiles(s):
    return dict(proj_tm=min(1024, s), out_tm=min(512, s), diff_tq=min(256, s),
                diff_tk=min(512, s), band_tq=min(256, s), merge_tm=min(512, s))


def _even_layer(x, pre_g, w_in, rpb, lam_q1, lam_k1, lam_q2, lam_k2, subln_g, w_out, post_g):
    s = x.shape[0]
    t = _tiles(s)
    tables, rope_half = _rope_tables(s, DIFF_QK_DIM)
    sections = [
        (ALL_PLAIN, HEAD_DIM ** -0.5), (ALL_PLAIN, 1.0), (ALL_PLAIN, 1.0), (ALL_PLAIN, 1.0),
        (ALL_ROPE, DIFF_QK_DIM ** -0.5), (ALL_ROPE, 1.0), (ALL_PLAIN, 1.0), (ALL_PLAIN, 1.0),
    ]
    proj = _inproj(x, pre_g, w_in.astype(BF16), tables, sections, rope_half, t["proj_tm"])
    mix_a = _natten(proj, rpb)
    vt = proj[6].reshape(s, N_HALF_HEADS, HEAD_DIM).transpose(1, 2, 0)
    lam_vecs = jnp.stack([lam_q1, lam_k1, lam_q2, lam_k2]).astype(F32)
    mix_b = _diff_attention(proj, vt, lam_vecs, subln_g, t["diff_tq"], t["diff_tk"])
    return _outproj(mix_a, mix_b, w_out.astype(BF16), post_g, x, t["out_tm"])


def _odd_layer(x, pre_g, w_in, sink, w_out, post_g):
    s = x.shape[0]
    t = _tiles(s)
    tables, rope_half = _rope_tables(s, HEAD_DIM)
    scale = HEAD_DIM ** -0.5
    kv_modes = (ROPE,) * D_KV_HEADS + (PLAIN,) * (CHUNKS - D_KV_HEADS)
    sections = (
        [(ALL_ROPE, scale)] * 3 + [(ALL_ROPE, 1.0)] * 3 + [(ALL_PLAIN, 1.0)] * 4
        + [(ALL_ROPE, scale), (kv_modes, 1.0), (ALL_PLAIN, 1.0)]
    )
    kv_w = 2 * D_KV_HEADS * HEAD_DIM
    c_w = 10 * SEC
    w_pad = jnp.concatenate([
        w_in[:, :c_w + SEC + kv_w],
        jnp.zeros((D_MODEL, SEC - kv_w), w_in.dtype),
        w_in[:, c_w + SEC + kv_w:]], axis=1).astype(BF16)
    proj = _inproj(x, pre_g, w_pad, tables, sections, rope_half, t["proj_tm"])
    outs, lses = [], []
    for gi, (_, dil) in enumerate(C_PATTERNS):
        o, lse = _dilated_group(proj, gi, dil, t["band_tq"])
        outs.append(o)
        lses.append(lse)
    mix_c = _merge_groups(outs, lses, proj, t["merge_tm"])
    mix_d = _windowed_sink(proj, sink, t["band_tq"])
    return _outproj(mix_c, mix_d, w_out.astype(BF16), post_g, x, t["out_tm"])


def kernel(x, l0_pre_g, l0_w_in, l0_rpb, l0_lam_q1, l0_lam_k1, l0_lam_q2, l0_lam_k2, l0_subln_g,
           l0_w_out, l0_post_g, l1_pre_g, l1_w_in, l1_sink, l1_w_out, l1_post_g):
    b, s, d = x.shape
    assert b == 1 and d == D_MODEL
    y = x.reshape(s, d)
    y = _even_layer(y, l0_pre_g, l0_w_in, l0_rpb, l0_lam_q1, l0_lam_k1, l0_lam_q2, l0_lam_k2,
                    l0_subln_g, l0_w_out, l0_post_g)
    y = _odd_layer(y, l1_pre_g, l1_w_in, l1_sink, l1_w_out, l1_post_g)
    return y.reshape(b, s, d)
```

```python
import functools
import math

import numpy as np
import jax
import jax.numpy as jnp
from jax import lax
from jax.experimental import pallas as pl
from jax.experimental.pallas import tpu as pltpu

F32 = jnp.float32
BF16 = jnp.bfloat16

D_MODEL = 2048
HEAD_DIM = 128
N_HALF_HEADS = 8
SEC = N_HALF_HEADS * HEAD_DIM
GRID_W = 64
ROW_WIN = 8
COL_WIN = 16
DIFF_QK_DIM = 64
LAMBDA_INIT = 0.2
C_PATTERNS = ((128, 1), (512, 4), (2048, 16))
D_KV_HEADS = 2
D_HALF_WINDOW = 128
ROPE_THETA = 500000.0
ROPE_FRACTION = 4
RMS_EPS = 1e-6
NEG_INF = -1e30
LOG2_E = math.log2(math.e)

LANES = 128
MXU_N = 256
VMEM_LIMIT_BYTES = 56 * 1024 * 1024

PLAIN, ROPE = 0, 1
CHUNKS = SEC // LANES
ALL_PLAIN = (PLAIN,) * CHUNKS
ALL_ROPE = (ROPE,) * CHUNKS
NORM_ROWS = 256
ROPE_LANES = 16
DIFF_ONES_ROWS = 16
FOLD_SECTIONS = 3

NT_DIMS = (((1,), (1,)), ((), ()))


def _silu(g):
    return g / (1.0 + jnp.exp(-g))


def _inproj_kernel(sections, n_main, fold_dils, x_ref, g_ref, w_ref, cos_ref, sin_ref, *rest):
    n_fold = len(fold_dils)
    o_ref = rest[0]
    fold_refs = rest[1:1 + n_fold]
    h_ref = rest[1 + n_fold]
    acc_ref = rest[2 + n_fold] if n_fold else None
    j = pl.program_id(1)
    tm = x_ref.shape[0]

    @pl.when(j == 0)
    def _():
        rc = min(NORM_ROWS, tm)
        for r in range(tm // rc):
            x = x_ref[r * rc:(r + 1) * rc, :]
            ms = jnp.mean(x * x, axis=-1, keepdims=True)
            h_ref[r * rc:(r + 1) * rc, :] = (x * lax.rsqrt(ms + RMS_EPS) * g_ref[...]).astype(BF16)

    def epilogue(a, cos, sin, mode, scale):
        if mode == ROPE:
            a = a * cos + pltpu.roll(a, LANES // 2, 1) * sin
        if scale != 1.0:
            a = a * scale
        return a.astype(BF16)

    groups = {}
    for idx, (modes, scale) in enumerate(sections):
        dest = -1 if idx < n_main else (idx - n_main) // FOLD_SECTIONS
        groups.setdefault((modes, scale, dest), []).append(idx)

    for (modes, scale, dest), idxs in groups.items():
        cond = functools.reduce(jnp.logical_or, [j == i for i in idxs])

        @pl.when(cond)
        def _(modes=modes, scale=scale, dest=dest):
            for nt in range(SEC // MXU_N):
                acc = jnp.dot(h_ref[...], w_ref[:, nt * MXU_N:(nt + 1) * MXU_N],
                              preferred_element_type=F32)
                chunks = range(nt * MXU_N // LANES, (nt + 1) * MXU_N // LANES)
                if dest < 0:
                    for c in chunks:
                        a = acc[:, (c * LANES) % MXU_N:(c * LANES) % MXU_N + LANES]
                        o_ref[:, c * LANES:(c + 1) * LANES] = epilogue(
                            a, cos_ref[...], sin_ref[...], modes[c], scale)
                else:
                    dil = fold_dils[dest]
                    for c in chunks:
                        acc_ref[c] = acc[:, (c * LANES) % MXU_N:(c * LANES) % MXU_N + LANES]
                    for r in range(dil):
                        rows = pl.ds(r, tm // dil, stride=dil)
                        for c in chunks:
                            a = acc_ref[c, rows, :]
                            fold_refs[dest][r, :, c * LANES:(c + 1) * LANES] = epilogue(
                                a, cos_ref[rows, :], sin_ref[rows, :], modes[c], scale)


def _inproj(x, pre_g, w, tables, sections, n_main, fold_dils, tm):
    s = x.shape[0]
    nsec = len(sections)
    assert nsec == n_main + FOLD_SECTIONS * len(fold_dils)
    cos_t, sin_t = tables
    tab_spec = pl.BlockSpec((tm, LANES), lambda i, j: (i, 0))
    out_specs = [pl.BlockSpec((None, tm, SEC), lambda i, j: (jnp.minimum(j, n_main - 1), i, 0))]
    out_shape = [jax.ShapeDtypeStruct((n_main, s, SEC), BF16)]
    scratch = [pltpu.VMEM((tm, D_MODEL), BF16)]
    for f, dil in enumerate(fold_dils):
        first = n_main + f * FOLD_SECTIONS
        out_specs.append(pl.BlockSpec(
            (None, dil, tm // dil, SEC),
            lambda i, j, first=first: (jnp.clip(j - first, 0, FOLD_SECTIONS - 1), 0, i, 0)))
        out_shape.append(jax.ShapeDtypeStruct((FOLD_SECTIONS, dil, s // dil, SEC), BF16))
    if fold_dils:
        scratch.append(pltpu.VMEM((CHUNKS, tm, LANES), F32))
    return pl.pallas_call(
        functools.partial(_inproj_kernel, tuple(sections), n_main, tuple(fold_dils)),
        grid=(s // tm, nsec),
        in_specs=[
            pl.BlockSpec((tm, D_MODEL), lambda i, j: (i, 0)),
            pl.BlockSpec((1, D_MODEL), lambda i, j: (0, 0)),
            pl.BlockSpec((D_MODEL, SEC), lambda i, j: (0, j)),
            tab_spec, tab_spec,
        ],
        out_specs=out_specs,
        out_shape=out_shape,
        scratch_shapes=scratch,
        compiler_params=pltpu.CompilerParams(
            dimension_semantics=("arbitrary", "arbitrary"),
            vmem_limit_bytes=VMEM_LIMIT_BYTES),
        name="inproj",
    )(x, pre_g.reshape(1, D_MODEL), w, cos_t, sin_t)


def _rope_tables(s, comp_dim):
    half = comp_dim // ROPE_FRACTION // 2
    n_comp = HEAD_DIM // comp_dim
    assert half * n_comp == ROPE_LANES
    inv = ROPE_THETA ** (-jnp.arange(half, dtype=F32) / half)
    ang = jnp.arange(s).astype(F32)[:, None] * inv[None, :]
    cos = jnp.tile(jnp.cos(ang), (1, n_comp))
    sin = jnp.tile(jnp.sin(ang), (1, n_comp))
    ones = jnp.ones((s, LANES // 2 - ROPE_LANES), F32)
    zeros = jnp.zeros((s, LANES // 2 - ROPE_LANES), F32)
    cos_t = jnp.concatenate([cos, ones, cos, ones], axis=1)
    sin_t = jnp.concatenate([-sin, zeros, sin, zeros], axis=1)
    return cos_t, sin_t


def _permute_head_dims(w, comp_dim):
    half = comp_dim // ROPE_FRACTION // 2
    d, width = w.shape
    nblk = HEAD_DIM // half
    w4 = w.reshape(d, width // HEAD_DIM, nblk, half)
    if comp_dim == HEAD_DIM:
        order = [(0, 1), (2, 5), (1, 2), (5, 8)]
    else:
        order = [(0, 1), (8, 9), (2, 5), (10, 13), (1, 2), (9, 10), (5, 8), (13, 16)]
    return jnp.concatenate([w4[:, :, a:b, :] for a, b in order], axis=2).reshape(d, width)


NA_QROWS = 4
NA_KROWS = 12
NA_TQ = NA_QROWS * GRID_W
NA_KBLK = 3


def _natten_kernel(q_ref, k0_ref, k1_ref, k2_ref, v0_ref, v1_ref, v2_ref, g_ref, bias_ref,
                   o_ref):
    k_refs = (k0_ref, k1_ref, k2_ref)
    v_refs = (v0_ref, v1_ref, v2_ref)
    for h in range(N_HALF_HEADS):
        cols = slice(h * HEAD_DIM, (h + 1) * HEAD_DIM)
        q = q_ref[:, cols]
        scores = []
        for t in range(NA_KBLK):
            st = lax.dot_general(q, k_refs[t][:, cols], NT_DIMS, preferred_element_type=F32)
            scores.append(st + bias_ref[h, :, t * NA_TQ:(t + 1) * NA_TQ])
        m = functools.reduce(jnp.maximum, [jnp.max(st, axis=-1, keepdims=True) for st in scores])
        ps = [jnp.exp(st - m) for st in scores]
        l = functools.reduce(jnp.add, [jnp.sum(p, axis=-1, keepdims=True) for p in ps])
        o = functools.reduce(jnp.add, [
            jnp.dot(ps[t].astype(BF16), v_refs[t][:, cols], preferred_element_type=F32)
            for t in range(NA_KBLK)])
        o = o / l
        o_ref[:, cols] = (o * _silu(g_ref[:, cols].astype(F32))).astype(BF16)


def _natten_bias_table(rpb, rows):
    n_heads, n_dr, n_dc = rpb.shape
    w = GRID_W
    wr = min(ROW_WIN, rows)
    nblk = rows // NA_QROWS
    lo = w - COL_WIN
    vec = jnp.pad(rpb, ((0, 0), (0, 0), (lo, 2 * w - lo - n_dc)))
    skew = jnp.tile(vec, (1, 1, w))[..., :w * (2 * w - 1)].reshape(n_heads, n_dr, w, 2 * w - 1)
    toep = skew[..., w - 1:2 * w - 1]
    qc = np.arange(w)[:, None]
    kc = np.arange(w)[None, :]
    cs = np.clip(qc - COL_WIN // 2, 0, w - COL_WIN)
    col_ok = (kc >= cs) & (kc < cs + COL_WIN)
    toep = jnp.where(jnp.asarray(col_ok), toep, NEG_INF)
    base = jnp.concatenate([toep, jnp.full((n_heads, 1, w, w), NEG_INF, F32)], axis=1)
    invalid = n_dr
    tabs = []
    for blk in (0, 1, nblk - 1):
        r0 = blk * NA_QROWS
        ws = min(max(blk - 1, 0), nblk - NA_KBLK) * NA_QROWS
        tiles = []
        for i in range(NA_QROWS):
            r = r0 + i
            rs = min(max(r - ROW_WIN // 2, 0), rows - wr)
            for jj in range(NA_KROWS):
                kr = ws + jj
                dr = kr - r + ROW_WIN - 1 if rs <= kr < rs + wr else invalid
                tiles.append(base[:, dr])
        tab = jnp.stack(tiles, axis=1).reshape(n_heads, NA_QROWS, NA_KROWS, w, w)
        tabs.append(tab.transpose(0, 1, 3, 2, 4).reshape(n_heads, NA_TQ, NA_KBLK * NA_TQ))
    return jnp.stack(tabs, axis=0)


def _natten(proj, rpb):
    s = proj.shape[1]
    rows = s // GRID_W
    nblk = rows // NA_QROWS
    bias = _natten_bias_table(rpb.astype(F32), rows)

    def kv_spec(sec, t):
        return pl.BlockSpec(
            (None, NA_TQ, SEC),
            lambda b: (sec, jnp.clip(b - 1, 0, nblk - NA_KBLK) + t, 0))

    def bias_idx(b):
        return (jnp.where(b == 0, 0, jnp.where(b == nblk - 1, 2, 1)), 0, 0, 0)

    return pl.pallas_call(
        _natten_kernel,
        grid=(nblk,),
        in_specs=[
            pl.BlockSpec((None, NA_TQ, SEC), lambda b: (0, b, 0)),
            kv_spec(1, 0), kv_spec(1, 1), kv_spec(1, 2),
            kv_spec(2, 0), kv_spec(2, 1), kv_spec(2, 2),
            pl.BlockSpec((None, NA_TQ, SEC), lambda b: (3, b, 0)),
            pl.BlockSpec((None, N_HALF_HEADS, NA_TQ, NA_KBLK * NA_TQ), bias_idx),
        ],
        out_specs=pl.BlockSpec((NA_TQ, SEC), lambda b: (b, 0)),
        out_shape=jax.ShapeDtypeStruct((s, SEC), BF16),
        compiler_params=pltpu.CompilerParams(
            dimension_semantics=("arbitrary",), vmem_limit_bytes=VMEM_LIMIT_BYTES),
        name="natten",
    )(proj, proj, proj, proj, proj, proj, proj, proj, bias)


def _diff_kernel(tq, tk, q_ref, k_ref, vt_ref, g_ref, lam_ref, subg_ref, o_ref,
                 qbd_ref, s_ref, m_ref, acc_ref):
    s = k_ref.shape[0]
    n_chunks = s // tk
    q = q_ref[...]
    l64 = lax.broadcasted_iota(jnp.int32, (tq, HEAD_DIM), 1) % (LANES // 2)
    first = (l64 < ROPE_LANES // 2) | ((l64 >= ROPE_LANES) & (l64 < ROPE_LANES + 24))
    zero = jnp.zeros_like(q)
    qbd_ref[0:tq, :] = jnp.where(first, q, zero)
    qbd_ref[tq:2 * tq, :] = jnp.where(first, zero, q)
    m_ref[...] = jnp.full(m_ref.shape, NEG_INF, F32)
    acc_ref[...] = jnp.zeros(acc_ref.shape, F32)

    def scores(chunk):
        start = pl.multiple_of(chunk * tk, tk)
        k = k_ref[pl.ds(start, tk), :]
        return lax.dot_general(k, qbd_ref[...], NT_DIMS, preferred_element_type=F32)

    def accumulate(slot, chunk):
        st = s_ref[slot]
        m_prev = m_ref[...]
        m_new = jnp.maximum(m_prev, jnp.max(st, axis=0, keepdims=True))
        alpha = jnp.exp2(m_prev - m_new)
        p = jnp.exp2(st - m_new).astype(BF16)
        start = pl.multiple_of(chunk * tk, tk)
        vt = vt_ref[:, pl.ds(start, tk)]
        acc_ref[...] = alpha * acc_ref[...] + jnp.dot(vt, p, preferred_element_type=F32)
        m_ref[...] = m_new

    s_ref[0] = scores(0)

    def body(i, carry):
        c0 = 2 * i
        s_ref[1] = scores(c0 + 1)
        accumulate(0, c0)
        s_ref[0] = scores(jnp.minimum(c0 + 2, n_chunks - 1))
        accumulate(1, c0 + 1)
        return carry

    lax.fori_loop(0, n_chunks // 2, body, 0)

    lam_v = lam_ref[...]
    lam = (jnp.exp(jnp.sum(lam_v[0:1] * lam_v[1:2], axis=-1, keepdims=True))
           - jnp.exp(jnp.sum(lam_v[2:3] * lam_v[3:4], axis=-1, keepdims=True))
           + LAMBDA_INIT)
    on = acc_ref[0:HEAD_DIM, :] / acc_ref[HEAD_DIM:HEAD_DIM + 1, :]
    ot = on[:, 0:tq] - lam * on[:, tq:2 * tq]
    ms = jnp.mean(ot * ot, axis=0, keepdims=True)
    ot = ot * lax.rsqrt(ms + RMS_EPS) * subg_ref[...] * (1.0 - LAMBDA_INIT)
    o = ot.T
    o_ref[...] = (o * _silu(g_ref[...].astype(F32))).astype(BF16)


def _diff_attention(proj, vt, lam_vecs, subln_g, tq, tk):
    s = proj.shape[1]
    vt_rows = vt.shape[1]
    assert (s // tk) % 2 == 0
    return pl.pallas_call(
        functools.partial(_diff_kernel, tq, tk),
        grid=(N_HALF_HEADS, s // tq),
        in_specs=[
            pl.BlockSpec((None, tq, HEAD_DIM), lambda h, i: (4, i, h)),
            pl.BlockSpec((None, s, HEAD_DIM), lambda h, i: (5, 0, h)),
            pl.BlockSpec((None, vt_rows, s), lambda h, i: (h, 0, 0)),
            pl.BlockSpec((None, tq, HEAD_DIM), lambda h, i: (7, i, h)),
            pl.BlockSpec((4, DIFF_QK_DIM), lambda h, i: (0, 0)),
            pl.BlockSpec((HEAD_DIM, 1), lambda h, i: (0, 0)),
        ],
        out_specs=pl.BlockSpec((tq, HEAD_DIM), lambda h, i: (i, h)),
        out_shape=jax.ShapeDtypeStruct((s, SEC), BF16),
        scratch_shapes=[
            pltpu.VMEM((2 * tq, HEAD_DIM), BF16),
            pltpu.VMEM((2, tk, 2 * tq), F32),
            pltpu.VMEM((1, 2 * tq), F32),
            pltpu.VMEM((vt_rows, 2 * tq), F32),
        ],
        compiler_params=pltpu.CompilerParams(
            dimension_semantics=("arbitrary", "arbitrary"),
            vmem_limit_bytes=VMEM_LIMIT_BYTES),
        name="diff_attn",
    )(proj, proj, vt, proj, lam_vecs, subln_g.reshape(HEAD_DIM, 1))


def _banded_kernel(tq, half, kcols, vcols, with_sink, *refs):
    if with_sink:
        (sink_ref, q_ref, km_ref, kp_ref, kn_ref, vm_ref, vp_ref, vn_ref, g_ref, o_ref) = refs
    else:
        (q_ref, km_ref, kp_ref, kn_ref, vm_ref, vp_ref, vn_ref, o_ref, lse_ref) = refs
    lb = pl.program_id(1)
    nblk = pl.num_programs(1)

    qi = lax.broadcasted_iota(jnp.int32, (tq, tq), 0)
    kj = lax.broadcasted_iota(jnp.int32, (tq, tq), 1)
    ok_main = jnp.abs(qi - kj) <= half
    qh = lax.broadcasted_iota(jnp.int32, (tq, 2 * half), 0)
    jh = lax.broadcasted_iota(jnp.int32, (tq, 2 * half), 1)
    prev_lo = jnp.where(lb > 0, 0, half)
    next_hi = jnp.where(lb < nblk - 1, 2 * half, half)
    ok_prev = (jh >= prev_lo) & (jh < half) & (qh <= jh)
    ok_next = (jh >= half) & (jh < next_hi) & (tq + (jh - half) - qh <= half)
    ok_halo = ok_prev | ok_next

    if not with_sink:
        lse_ref[...] = jnp.zeros(lse_ref.shape, F32)

    for h in range(N_HALF_HEADS):
        cols = slice(h * HEAD_DIM, (h + 1) * HEAD_DIM)
        kc = slice(kcols[h], kcols[h] + HEAD_DIM)
        vc = slice(vcols[h], vcols[h] + HEAD_DIM)
        q = q_ref[:, cols]
        k_halo = jnp.concatenate([kp_ref[:, kc], kn_ref[:, kc]], axis=0)
        v_halo = jnp.concatenate([vp_ref[:, vc], vn_ref[:, vc]], axis=0)
        s_m = lax.dot_general(q, km_ref[:, kc], NT_DIMS, preferred_element_type=F32)
        s_h = lax.dot_general(q, k_halo, NT_DIMS, preferred_element_type=F32)
        s_m = jnp.where(ok_main, s_m, NEG_INF)
        s_h = jnp.where(ok_halo, s_h, NEG_INF)
        m = jnp.maximum(jnp.max(s_m, axis=-1, keepdims=True), jnp.max(s_h, axis=-1, keepdims=True))
        p_m = jnp.exp(s_m - m)
        p_h = jnp.exp(s_h - m)
        l = jnp.sum(p_m, axis=-1, keepdims=True) + jnp.sum(p_h, axis=-1, keepdims=True)
        acc = (jnp.dot(p_m.astype(BF16), vm_ref[:, vc], preferred_element_type=F32)
               + jnp.dot(p_h.astype(BF16), v_halo, preferred_element_type=F32))
        if with_sink:
            o = acc / (l + jnp.exp(sink_ref[h] - m))
            o_ref[:, cols] = (o * _silu(g_ref[:, cols].astype(F32))).astype(BF16)
        else:
            o_ref[:, cols] = acc / l
            lse_ref[:, h:h + 1] = m + jnp.log(l)


def _banded_specs(tq, half, q_sec, k_sec, v_sec, nblk):
    r = tq // half

    def main(sec):
        return pl.BlockSpec((None, None, tq, SEC), lambda c, lb, *_: (sec, c, lb, 0))

    def prev(sec):
        return pl.BlockSpec((None, None, half, SEC),
                            lambda c, lb, *_: (sec, c, jnp.maximum(lb * r - 1, 0), 0))

    def nxt(sec):
        return pl.BlockSpec((None, None, half, SEC),
                            lambda c, lb, *_: (sec, c, jnp.minimum((lb + 1) * r, nblk * r - 1), 0))

    return [main(q_sec), main(k_sec), prev(k_sec), nxt(k_sec),
            main(v_sec), prev(v_sec), nxt(v_sec)]


def _dilated_group(qkv, half, tq):
    _, dil, length, _ = qkv.shape
    tq = min(tq, length)
    nblk = length // tq
    cols = tuple(h * HEAD_DIM for h in range(N_HALF_HEADS))
    return pl.pallas_call(
        functools.partial(_banded_kernel, tq, half, cols, cols, False),
        grid=(dil, nblk),
        in_specs=_banded_specs(tq, half, 0, 1, 2, nblk),
        out_specs=[pl.BlockSpec((None, tq, SEC), lambda c, lb: (c, lb, 0)),
                   pl.BlockSpec((None, tq, LANES), lambda c, lb: (c, lb, 0))],
        out_shape=[jax.ShapeDtypeStruct((dil, length, SEC), F32),
                   jax.ShapeDtypeStruct((dil, length, LANES), F32)],
        compiler_params=pltpu.CompilerParams(
            dimension_semantics=("arbitrary", "arbitrary"),
            vmem_limit_bytes=VMEM_LIMIT_BYTES),
        name=f"dilated_x{dil}",
    )(*([qkv] * 7))


def _windowed_sink(proj, sink, tq):
    s = proj.shape[2]
    half = D_HALF_WINDOW
    nblk = s // tq
    grp = N_HALF_HEADS // D_KV_HEADS
    kcols = tuple((h // grp) * HEAD_DIM for h in range(N_HALF_HEADS))
    vcols = tuple(D_KV_HEADS * HEAD_DIM + (h // grp) * HEAD_DIM for h in range(N_HALF_HEADS))
    specs = _banded_specs(tq, half, 4, 5, 5, nblk)
    specs.append(pl.BlockSpec((None, None, tq, SEC), lambda c, lb, *_: (6, c, lb, 0)))
    grid_spec = pltpu.PrefetchScalarGridSpec(
        num_scalar_prefetch=1,
        grid=(1, nblk),
        in_specs=specs,
        out_specs=pl.BlockSpec((tq, SEC), lambda c, lb, *_: (lb, 0)),
    )
    return pl.pallas_call(
        functools.partial(_banded_kernel, tq, half, kcols, vcols, True),
        grid_spec=grid_spec,
        out_shape=jax.ShapeDtypeStruct((s, SEC), BF16),
        compiler_params=pltpu.CompilerParams(
            dimension_semantics=("arbitrary", "arbitrary"),
            vmem_limit_bytes=VMEM_LIMIT_BYTES),
        name="windowed_sink",
    )(sink.astype(F32), *([proj] * 8))


def _merge_kernel(dils, o0_ref, o1_ref, o2_ref, l0_ref, l1_ref, l2_ref, g_ref, out_ref,
                  *scratch):
    tm = out_ref.shape[0]
    o_refs = [o0_ref, o1_ref, o2_ref]
    l_refs = [l0_ref, l1_ref, l2_ref]
    head_out, l_tok = [], []
    for gi, dil in enumerate(dils):
        if dil == 1:
            head_out.append(lambda h, r=o_refs[gi]: r[0, :, h * HEAD_DIM:(h + 1) * HEAD_DIM])
            l_tok.append(l_refs[gi].at[0])
            continue
        ou_ref, lu_ref = scratch[2 * (gi - 1)], scratch[2 * (gi - 1) + 1]
        for c in range(dil):
            rows = pl.ds(c, tm // dil, stride=dil)
            for h in range(N_HALF_HEADS):
                ou_ref[h, rows, :] = o_refs[gi][c, :, h * HEAD_DIM:(h + 1) * HEAD_DIM]
            lu_ref[rows, :] = l_refs[gi][c]
        head_out.append(lambda h, r=ou_ref: r[h])
        l_tok.append(lu_ref)
    for h in range(N_HALF_HEADS):
        cols = slice(h * HEAD_DIM, (h + 1) * HEAD_DIM)
        ls = [r[:, h:h + 1] for r in l_tok]
        m = jnp.maximum(jnp.maximum(ls[0], ls[1]), ls[2])
        es = [jnp.exp(x - m) for x in ls]
        den = es[0] + es[1] + es[2]
        o = ((es[0] / den) * head_out[0](h) + (es[1] / den) * head_out[1](h)
             + (es[2] / den) * head_out[2](h))
        out_ref[:, cols] = (o * _silu(g_ref[:, cols].astype(F32))).astype(BF16)


def _merge_groups(outs, lses, proj, gate_sec, tm):
    s = proj.shape[1]
    dils = tuple(o.shape[0] for o in outs)
    assert dils[0] == 1
    in_specs = [pl.BlockSpec((d, tm // d, SEC), lambda i: (0, i, 0)) for d in dils]
    in_specs += [pl.BlockSpec((d, tm // d, LANES), lambda i: (0, i, 0)) for d in dils]
    in_specs.append(pl.BlockSpec((None, tm, SEC), lambda i: (gate_sec, i, 0)))
    scratch = []
    for d in dils[1:]:
        scratch += [pltpu.VMEM((N_HALF_HEADS, tm, HEAD_DIM), F32), pltpu.VMEM((tm, LANES), F32)]
    return pl.pallas_call(
        functools.partial(_merge_kernel, dils),
        grid=(s // tm,),
        in_specs=in_specs,
        out_specs=pl.BlockSpec((tm, SEC), lambda i: (i, 0)),
        out_shape=jax.ShapeDtypeStruct((s, SEC), BF16),
        scratch_shapes=scratch,
        compiler_params=pltpu.CompilerParams(
            dimension_semantics=("arbitrary",), vmem_limit_bytes=VMEM_LIMIT_BYTES),
        name="merge_groups",
    )(*outs, *lses, proj)


def _outproj_kernel(a_ref, b_ref, w_ref, g_ref, x_ref, o_ref):
    y = (jnp.dot(a_ref[...], w_ref[0:SEC, :], preferred_element_type=F32)
         + jnp.dot(b_ref[...], w_ref[SEC:2 * SEC, :], preferred_element_type=F32))
    ms = jnp.mean(y * y, axis=-1, keepdims=True)
    o_ref[...] = x_ref[...] + y * lax.rsqrt(ms + RMS_EPS) * g_ref[...]


def _outproj(mix_a, mix_b, w_out, post_g, x, tm):
    s = x.shape[0]
    half_spec = pl.BlockSpec((tm, SEC), lambda i: (i, 0))
    return pl.pallas_call(
        _outproj_kernel,
        grid=(s // tm,),
        in_specs=[half_spec, half_spec,
                  pl.BlockSpec((2 * SEC, D_MODEL), lambda i: (0, 0)),
                  pl.BlockSpec((1, D_MODEL), lambda i: (0, 0)),
                  pl.BlockSpec((tm, D_MODEL), lambda i: (i, 0))],
        out_specs=pl.BlockSpec((tm, D_MODEL), lambda i: (i, 0)),
        out_shape=jax.ShapeDtypeStruct((s, D_MODEL), F32),
        compiler_params=pltpu.CompilerParams(
            dimension_semantics=("arbitrary",), vmem_limit_bytes=VMEM_LIMIT_BYTES),
        name="outproj",
    )(mix_a, mix_b, w_out, post_g.reshape(1, D_MODEL), x)


def _tiles(s):
    return dict(proj_tm=min(1024, s), out_tm=min(512, s), diff_tq=min(512, s),
                diff_tk=min(1024, s // 2), band_tq=min(256, s), merge_tm=min(512, s))


def _even_layer(x, pre_g, w_in, rpb, lam_q1, lam_k1, lam_q2, lam_k2, subln_g, w_out, post_g):
    s = x.shape[0]
    t = _tiles(s)
    tables = _rope_tables(s, DIFF_QK_DIM)
    sections = [
        (ALL_PLAIN, HEAD_DIM ** -0.5), (ALL_PLAIN, 1.0), (ALL_PLAIN, 1.0), (ALL_PLAIN, 1.0),
        (ALL_ROPE, DIFF_QK_DIM ** -0.5 * LOG2_E), (ALL_ROPE, 1.0), (ALL_PLAIN, 1.0), (ALL_PLAIN, 1.0),
    ]
    w = jnp.concatenate([
        w_in[:, :4 * SEC],
        _permute_head_dims(w_in[:, 4 * SEC:5 * SEC], DIFF_QK_DIM),
        _permute_head_dims(w_in[:, 5 * SEC:6 * SEC], DIFF_QK_DIM),
        w_in[:, 6 * SEC:]], axis=1).astype(BF16)
    (proj,) = _inproj(x, pre_g, w, tables, sections, len(sections), (), t["proj_tm"])
    mix_a = _natten(proj, rpb)
    vt = proj[6].reshape(s, N_HALF_HEADS, HEAD_DIM).transpose(1, 2, 0)
    vt = jnp.concatenate([vt, jnp.ones((N_HALF_HEADS, DIFF_ONES_ROWS, s), BF16)], axis=1)
    lam_vecs = jnp.stack([lam_q1, lam_k1, lam_q2, lam_k2]).astype(F32)
    mix_b = _diff_attention(proj, vt, lam_vecs, subln_g, t["diff_tq"], t["diff_tk"])
    return _outproj(mix_a, mix_b, w_out.astype(BF16), post_g, x, t["out_tm"])


def _odd_layer(x, pre_g, w_in, sink, w_out, post_g):
    s = x.shape[0]
    t = _tiles(s)
    tables = _rope_tables(s, HEAD_DIM)
    scale = HEAD_DIM ** -0.5
    n_groups = len(C_PATTERNS)
    kv_w = D_KV_HEADS * HEAD_DIM
    qc, kc, vc = (w_in[:, i * n_groups * SEC:(i + 1) * n_groups * SEC] for i in range(3))
    rest = w_in[:, 3 * n_groups * SEC:]
    gc, qd, kd, vd, gd = (rest[:, :SEC], rest[:, SEC:2 * SEC], rest[:, 2 * SEC:2 * SEC + kv_w],
                          rest[:, 2 * SEC + kv_w:2 * SEC + 2 * kv_w], rest[:, 2 * SEC + 2 * kv_w:])

    def group(w3, gi):
        return w3[:, gi * SEC:(gi + 1) * SEC]

    def perm(wq):
        return _permute_head_dims(wq, HEAD_DIM)

    kvd = jnp.concatenate([perm(kd), vd, jnp.zeros((D_MODEL, SEC - 2 * kv_w), w_in.dtype)], axis=1)
    blocks = [perm(group(qc, 0)), perm(group(kc, 0)), group(vc, 0), gc, perm(qd), kvd, gd]
    kv_modes = (ROPE,) * D_KV_HEADS + (PLAIN,) * (CHUNKS - D_KV_HEADS)
    sections = [(ALL_ROPE, scale), (ALL_ROPE, 1.0), (ALL_PLAIN, 1.0), (ALL_PLAIN, 1.0),
                (ALL_ROPE, scale), (kv_modes, 1.0), (ALL_PLAIN, 1.0)]
    n_main = len(sections)
    fold_dils = tuple(dil for _, dil in C_PATTERNS[1:])
    for gi in range(1, n_groups):
        blocks += [perm(group(qc, gi)), perm(group(kc, gi)), group(vc, gi)]
        sections += [(ALL_ROPE, scale), (ALL_ROPE, 1.0), (ALL_PLAIN, 1.0)]
    w = jnp.concatenate(blocks, axis=1).astype(BF16)
    proj, *folds = _inproj(x, pre_g, w, tables, sections, n_main, fold_dils, t["proj_tm"])
    proj4 = proj.reshape(n_main, 1, s, SEC)
    outs, lses = [], []
    for gi, (window, dil) in enumerate(C_PATTERNS):
        o, lse = _dilated_group(proj4 if gi == 0 else folds[gi - 1], (window // 2) // dil,
                                t["band_tq"])
        outs.append(o)
        lses.append(lse)
    mix_c = _merge_groups(outs, lses, proj, 3, t["merge_tm"])
    mix_d = _windowed_sink(proj4, sink, t["band_tq"])
    return _outproj(mix_c, mix_d, w_out.astype(BF16), post_g, x, t["out_tm"])


def kernel(x, l0_pre_g, l0_w_in, l0_rpb, l0_lam_q1, l0_lam_k1, l0_lam_q2, l0_lam_k2, l0_subln_g,
           l0_w_out, l0_post_g, l1_pre_g, l1_w_in, l1_sink, l1_w_out, l1_post_g):
    b, s, d = x.shape
    assert b == 1 and d == D_MODEL
    y = x.reshape(s, d)
    y = _even_layer(y, l0_pre_g, l0_w_in, l0_rpb, l0_lam_q1, l0_lam_k1, l0_lam_q2, l0_lam_k2,
                    l0_subln_g, l0_w_out, l0_post_g)
    y = _odd_layer(y, l1_pre_g, l1_w_in, l1_sink, l1_w_out, l1_post_g)
    return y.reshape(b, s, d)
```

```python
import functools
import math

import numpy as np
import jax
import jax.numpy as jnp
from jax import lax
from jax.experimental import pallas as pl
from jax.experimental.pallas import tpu as pltpu

F32 = jnp.float32
BF16 = jnp.bfloat16

D_MODEL = 2048
HEAD_DIM = 128
N_HALF_HEADS = 8
SEC = N_HALF_HEADS * HEAD_DIM
GRID_W = 64
ROW_WIN = 8
COL_WIN = 16
DIFF_QK_DIM = 64
LAMBDA_INIT = 0.2
C_PATTERNS = ((128, 1), (512, 4), (2048, 16))
D_KV_HEADS = 2
D_HALF_WINDOW = 128
ROPE_THETA = 500000.0
ROPE_FRACTION = 4
RMS_EPS = 1e-6
NEG_INF = -1e30
LOG2_E = math.log2(math.e)

LANES = 128
MXU_N = 256
VMEM_LIMIT_BYTES = 56 * 1024 * 1024

PLAIN, ROPE = 0, 1
CHUNKS = SEC // LANES
ALL_PLAIN = (PLAIN,) * CHUNKS
ALL_ROPE = (ROPE,) * CHUNKS
NORM_ROWS = 256
ROPE_LANES = 16
DIFF_ONES_ROWS = 16
FOLD_SECTIONS = 3

NT_DIMS = (((1,), (1,)), ((), ()))


def _silu(g):
    return g / (1.0 + jnp.exp(-g))


SEC_PARAMS = 2 * CHUNKS + 1


def _inproj_kernel(n_main, fold_dils, par_ref, x_ref, g_ref, w_ref, cos_ref, sin_ref, *rest):
    n_fold = len(fold_dils)
    o_ref = rest[0]
    fold_refs = rest[1:1 + n_fold]
    h_ref = rest[1 + n_fold]
    acc_ref = rest[2 + n_fold] if n_fold else None
    j = pl.program_id(1)
    tm = x_ref.shape[0]

    @pl.when(j == 0)
    def _():
        rc = min(NORM_ROWS, tm)
        for r in range(tm // rc):
            x = x_ref[r * rc:(r + 1) * rc, :]
            ms = jnp.mean(x * x, axis=-1, keepdims=True)
            h_ref[r * rc:(r + 1) * rc, :] = (x * lax.rsqrt(ms + RMS_EPS) * g_ref[...]).astype(BF16)

    base = j * SEC_PARAMS
    scale = par_ref[base + 2 * CHUNKS]

    def epilogue(a, cos, sin, c):
        on = par_ref[base + c]
        off = par_ref[base + CHUNKS + c]
        a = a * (cos * on + off) + pltpu.roll(a, LANES // 2, 1) * (sin * on)
        return (a * scale).astype(BF16)

    def section(dest):
        for nt in range(SEC // MXU_N):
            acc = jnp.dot(h_ref[...], w_ref[:, nt * MXU_N:(nt + 1) * MXU_N],
                          preferred_element_type=F32)
            chunks = range(nt * MXU_N // LANES, (nt + 1) * MXU_N // LANES)
            if dest < 0:
                for c in chunks:
                    a = acc[:, (c * LANES) % MXU_N:(c * LANES) % MXU_N + LANES]
                    o_ref[:, c * LANES:(c + 1) * LANES] = epilogue(a, cos_ref[...], sin_ref[...], c)
            else:
                dil = fold_dils[dest]
                for c in chunks:
                    acc_ref[c] = acc[:, (c * LANES) % MXU_N:(c * LANES) % MXU_N + LANES]
                for r in range(dil):
                    rows = pl.ds(r, tm // dil, stride=dil)
                    for c in chunks:
                        fold_refs[dest][r, :, c * LANES:(c + 1) * LANES] = epilogue(
                            acc_ref[c, rows, :], cos_ref[rows, :], sin_ref[rows, :], c)

    if n_fold == 0:
        section(-1)
    else:
        pl.when(j < n_main)(functools.partial(section, -1))
        for f in range(n_fold):
            first = n_main + f * FOLD_SECTIONS
            pl.when((j >= first) & (j < first + FOLD_SECTIONS))(functools.partial(section, f))


def _inproj(x, pre_g, w, tables, sections, n_main, fold_dils, tm):
    s = x.shape[0]
    nsec = len(sections)
    assert nsec == n_main + FOLD_SECTIONS * len(fold_dils)
    cos_t, sin_t = tables
    params = []
    for modes, scale in sections:
        params += [float(m == ROPE) for m in modes] + [float(m != ROPE) for m in modes] + [scale]
    params = jnp.asarray(np.asarray(params, np.float32))
    tab_spec = pl.BlockSpec((tm, LANES), lambda i, j, p: (i, 0))
    out_specs = [pl.BlockSpec((None, tm, SEC), lambda i, j, p: (jnp.minimum(j, n_main - 1), i, 0))]
    out_shape = [jax.ShapeDtypeStruct((n_main, s, SEC), BF16)]
    scratch = [pltpu.VMEM((tm, D_MODEL), BF16)]
    for f, dil in enumerate(fold_dils):
        first = n_main + f * FOLD_SECTIONS
        out_specs.append(pl.BlockSpec(
            (None, dil, tm // dil, SEC),
            lambda i, j, p, first=first: (jnp.clip(j - first, 0, FOLD_SECTIONS - 1), 0, i, 0)))
        out_shape.append(jax.ShapeDtypeStruct((FOLD_SECTIONS, dil, s // dil, SEC), BF16))
    if fold_dils:
        scratch.append(pltpu.VMEM((CHUNKS, tm, LANES), F32))
    grid_spec = pltpu.PrefetchScalarGridSpec(
        num_scalar_prefetch=1,
        grid=(s // tm, nsec),
        in_specs=[
            pl.BlockSpec((tm, D_MODEL), lambda i, j, p: (i, 0)),
            pl.BlockSpec((1, D_MODEL), lambda i, j, p: (0, 0)),
            pl.BlockSpec((D_MODEL, SEC), lambda i, j, p: (0, j)),
            tab_spec, tab_spec,
        ],
        out_specs=out_specs,
        scratch_shapes=scratch,
    )
    return pl.pallas_call(
        functools.partial(_inproj_kernel, n_main, tuple(fold_dils)),
        grid_spec=grid_spec,
        out_shape=out_shape,
        compiler_params=pltpu.CompilerParams(
            dimension_semantics=("arbitrary", "arbitrary"),
            vmem_limit_bytes=VMEM_LIMIT_BYTES),
        name="inproj",
    )(params, x, pre_g.reshape(1, D_MODEL), w, cos_t, sin_t)


def _rope_tables(s, comp_dim):
    half = comp_dim // ROPE_FRACTION // 2
    n_comp = HEAD_DIM // comp_dim
    assert half * n_comp == ROPE_LANES
    inv = ROPE_THETA ** (-jnp.arange(half, dtype=F32) / half)
    ang = jnp.arange(s).astype(F32)[:, None] * inv[None, :]
    cos = jnp.tile(jnp.cos(ang), (1, n_comp))
    sin = jnp.tile(jnp.sin(ang), (1, n_comp))
    ones = jnp.ones((s, LANES // 2 - ROPE_LANES), F32)
    zeros = jnp.zeros((s, LANES // 2 - ROPE_LANES), F32)
    cos_t = jnp.concatenate([cos, ones, cos, ones], axis=1)
    sin_t = jnp.concatenate([-sin, zeros, sin, zeros], axis=1)
    return cos_t, sin_t


def _permute_head_dims(w, comp_dim):
    half = comp_dim // ROPE_FRACTION // 2
    d, width = w.shape
    nblk = HEAD_DIM // half
    w4 = w.reshape(d, width // HEAD_DIM, nblk, half)
    if comp_dim == HEAD_DIM:
        order = [(0, 1), (2, 5), (1, 2), (5, 8)]
    else:
        order = [(0, 1), (8, 9), (2, 5), (10, 13), (1, 2), (9, 10), (5, 8), (13, 16)]
    return jnp.concatenate([w4[:, :, a:b, :] for a, b in order], axis=2).reshape(d, width)


NA_QROWS = 4
NA_KROWS = 12
NA_TQ = NA_QROWS * GRID_W
NA_KBLK = 3


def _natten_kernel(q_ref, k0_ref, k1_ref, k2_ref, v0_ref, v1_ref, v2_ref, g_ref, bias_ref,
                   o_ref):
    k_refs = (k0_ref, k1_ref, k2_ref)
    v_refs = (v0_ref, v1_ref, v2_ref)
    for h in range(N_HALF_HEADS):
        cols = slice(h * HEAD_DIM, (h + 1) * HEAD_DIM)
        q = q_ref[:, cols]
        scores = []
        for t in range(NA_KBLK):
            st = lax.dot_general(q, k_refs[t][:, cols], NT_DIMS, preferred_element_type=F32)
            scores.append(st + bias_ref[h, :, t * NA_TQ:(t + 1) * NA_TQ])
        m = functools.reduce(jnp.maximum, [jnp.max(st, axis=-1, keepdims=True) for st in scores])
        ps = [jnp.exp(st - m) for st in scores]
        l = functools.reduce(jnp.add, [jnp.sum(p, axis=-1, keepdims=True) for p in ps])
        o = functools.reduce(jnp.add, [
            jnp.dot(ps[t].astype(BF16), v_refs[t][:, cols], preferred_element_type=F32)
            for t in range(NA_KBLK)])
        o = o / l
        o_ref[:, cols] = (o * _silu(g_ref[:, cols].astype(F32))).astype(BF16)


def _natten_bias_table(rpb, rows):
    n_heads, n_dr, n_dc = rpb.shape
    w = GRID_W
    wr = min(ROW_WIN, rows)
    nblk = rows // NA_QROWS
    lo = w - COL_WIN
    vec = jnp.pad(rpb, ((0, 0), (0, 0), (lo, 2 * w - lo - n_dc)))
    skew = jnp.tile(vec, (1, 1, w))[..., :w * (2 * w - 1)].reshape(n_heads, n_dr, w, 2 * w - 1)
    toep = skew[..., w - 1:2 * w - 1]
    qc = np.arange(w)[:, None]
    kc = np.arange(w)[None, :]
    cs = np.clip(qc - COL_WIN // 2, 0, w - COL_WIN)
    col_ok = (kc >= cs) & (kc < cs + COL_WIN)
    toep = jnp.where(jnp.asarray(col_ok), toep, NEG_INF)
    base = jnp.concatenate([toep, jnp.full((n_heads, 1, w, w), NEG_INF, F32)], axis=1)
    invalid = n_dr
    tabs = []
    for blk in (0, 1, nblk - 1):
        r0 = blk * NA_QROWS
        ws = min(max(blk - 1, 0), nblk - NA_KBLK) * NA_QROWS
        tiles = []
        for i in range(NA_QROWS):
            r = r0 + i
            rs = min(max(r - ROW_WIN // 2, 0), rows - wr)
            for jj in range(NA_KROWS):
                kr = ws + jj
                dr = kr - r + ROW_WIN - 1 if rs <= kr < rs + wr else invalid
                tiles.append(base[:, dr])
        tab = jnp.stack(tiles, axis=1).reshape(n_heads, NA_QROWS, NA_KROWS, w, w)
        tabs.append(tab.transpose(0, 1, 3, 2, 4).reshape(n_heads, NA_TQ, NA_KBLK * NA_TQ))
    return jnp.stack(tabs, axis=0)


def _natten(proj, rpb):
    s = proj.shape[1]
    rows = s // GRID_W
    nblk = rows // NA_QROWS
    bias = _natten_bias_table(rpb.astype(F32), rows)

    def kv_spec(sec, t):
        return pl.BlockSpec(
            (None, NA_TQ, SEC),
            lambda b: (sec, jnp.clip(b - 1, 0, nblk - NA_KBLK) + t, 0))

    def bias_idx(b):
        return (jnp.where(b == 0, 0, jnp.where(b == nblk - 1, 2, 1)), 0, 0, 0)

    return pl.pallas_call(
        _natten_kernel,
        grid=(nblk,),
        in_specs=[
            pl.BlockSpec((None, NA_TQ, SEC), lambda b: (0, b, 0)),
            kv_spec(1, 0), kv_spec(1, 1), kv_spec(1, 2),
            kv_spec(2, 0), kv_spec(2, 1), kv_spec(2, 2),
            pl.BlockSpec((None, NA_TQ, SEC), lambda b: (3, b, 0)),
            pl.BlockSpec((None, N_HALF_HEADS, NA_TQ, NA_KBLK * NA_TQ), bias_idx),
        ],
        out_specs=pl.BlockSpec((NA_TQ, SEC), lambda b: (b, 0)),
        out_shape=jax.ShapeDtypeStruct((s, SEC), BF16),
        compiler_params=pltpu.CompilerParams(
            dimension_semantics=("arbitrary",), vmem_limit_bytes=VMEM_LIMIT_BYTES),
        name="natten",
    )(proj, proj, proj, proj, proj, proj, proj, proj, bias)


def _diff_kernel(tq, tk, q_ref, k_ref, vt_ref, g_ref, lam_ref, subg_ref, o_ref,
                 qbd_ref, s_ref, m_ref, acc_ref):
    s = k_ref.shape[0]
    n_chunks = s // tk
    q = q_ref[...]
    l64 = lax.broadcasted_iota(jnp.int32, (tq, HEAD_DIM), 1) % (LANES // 2)
    first = (l64 < ROPE_LANES // 2) | ((l64 >= ROPE_LANES) & (l64 < ROPE_LANES + 24))
    zero = jnp.zeros_like(q)
    qbd_ref[0:tq, :] = jnp.where(first, q, zero)
    qbd_ref[tq:2 * tq, :] = jnp.where(first, zero, q)
    m_ref[...] = jnp.full(m_ref.shape, NEG_INF, F32)
    acc_ref[...] = jnp.zeros(acc_ref.shape, F32)

    def scores(chunk):
        start = pl.multiple_of(chunk * tk, tk)
        k = k_ref[pl.ds(start, tk), :]
        return lax.dot_general(k, qbd_ref[...], NT_DIMS, preferred_element_type=F32)

    def accumulate(slot, chunk):
        st = s_ref[slot]
        m_prev = m_ref[...]
        m_new = jnp.maximum(m_prev, jnp.max(st, axis=0, keepdims=True))
        alpha = jnp.exp2(m_prev - m_new)
        p = jnp.exp2(st - m_new).astype(BF16)
        start = pl.multiple_of(chunk * tk, tk)
        vt = vt_ref[:, pl.ds(start, tk)]
        acc_ref[...] = alpha * acc_ref[...] + jnp.dot(vt, p, preferred_element_type=F32)
        m_ref[...] = m_new

    s_ref[0] = scores(0)

    def body(i, carry):
        c0 = 2 * i
        s_ref[1] = scores(c0 + 1)
        accumulate(0, c0)
        s_ref[0] = scores(jnp.minimum(c0 + 2, n_chunks - 1))
        accumulate(1, c0 + 1)
        return carry

    lax.fori_loop(0, n_chunks // 2, body, 0)

    lam_v = lam_ref[...]
    lam = (jnp.exp(jnp.sum(lam_v[0:1] * lam_v[1:2], axis=-1, keepdims=True))
           - jnp.exp(jnp.sum(lam_v[2:3] * lam_v[3:4], axis=-1, keepdims=True))
           + LAMBDA_INIT)
    on = acc_ref[0:HEAD_DIM, :] / acc_ref[HEAD_DIM:HEAD_DIM + 1, :]
    ot = on[:, 0:tq] - lam * on[:, tq:2 * tq]
    ms = jnp.mean(ot * ot, axis=0, keepdims=True)
    ot = ot * lax.rsqrt(ms + RMS_EPS) * subg_ref[...] * (1.0 - LAMBDA_INIT)
    o = ot.T
    o_ref[...] = (o * _silu(g_ref[...].astype(F32))).astype(BF16)


def _diff_attention(proj, vt, lam_vecs, subln_g, tq, tk):
    s = proj.shape[1]
    vt_rows = vt.shape[1]
    assert (s // tk) % 2 == 0
    return pl.pallas_call(
        functools.partial(_diff_kernel, tq, tk),
        grid=(N_HALF_HEADS, s // tq),
        in_specs=[
            pl.BlockSpec((None, tq, HEAD_DIM), lambda h, i: (4, i, h)),
            pl.BlockSpec((None, s, HEAD_DIM), lambda h, i: (5, 0, h)),
            pl.BlockSpec((None, vt_rows, s), lambda h, i: (h, 0, 0)),
            pl.BlockSpec((None, tq, HEAD_DIM), lambda h, i: (7, i, h)),
            pl.BlockSpec((4, DIFF_QK_DIM), lambda h, i: (0, 0)),
            pl.BlockSpec((HEAD_DIM, 1), lambda h, i: (0, 0)),
        ],
        out_specs=pl.BlockSpec((tq, HEAD_DIM), lambda h, i: (i, h)),
        out_shape=jax.ShapeDtypeStruct((s, SEC), BF16),
        scratch_shapes=[
            pltpu.VMEM((2 * tq, HEAD_DIM), BF16),
            pltpu.VMEM((2, tk, 2 * tq), F32),
            pltpu.VMEM((1, 2 * tq), F32),
            pltpu.VMEM((vt_rows, 2 * tq), F32),
        ],
        compiler_params=pltpu.CompilerParams(
            dimension_semantics=("arbitrary", "arbitrary"),
            vmem_limit_bytes=VMEM_LIMIT_BYTES),
        name="diff_attn",
    )(proj, proj, vt, proj, lam_vecs, subln_g.reshape(HEAD_DIM, 1))


def _banded_kernel(tq, half, kcols, vcols, with_sink, *refs):
    if with_sink:
        (sink_ref, q_ref, km_ref, kp_ref, kn_ref, vm_ref, vp_ref, vn_ref, g_ref, o_ref) = refs
    else:
        (q_ref, km_ref, kp_ref, kn_ref, vm_ref, vp_ref, vn_ref, o_ref, lse_ref) = refs
    lb = pl.program_id(1)
    nblk = pl.num_programs(1)

    qi = lax.broadcasted_iota(jnp.int32, (tq, tq), 0)
    kj = lax.broadcasted_iota(jnp.int32, (tq, tq), 1)
    ok_main = jnp.abs(qi - kj) <= half
    qh = lax.broadcasted_iota(jnp.int32, (tq, 2 * half), 0)
    jh = lax.broadcasted_iota(jnp.int32, (tq, 2 * half), 1)
    prev_lo = jnp.where(lb > 0, 0, half)
    next_hi = jnp.where(lb < nblk - 1, 2 * half, half)
    ok_prev = (jh >= prev_lo) & (jh < half) & (qh <= jh)
    ok_next = (jh >= half) & (jh < next_hi) & (tq + (jh - half) - qh <= half)
    ok_halo = ok_prev | ok_next

    if not with_sink:
        lse_ref[...] = jnp.zeros(lse_ref.shape, F32)

    for h in range(N_HALF_HEADS):
        cols = slice(h * HEAD_DIM, (h + 1) * HEAD_DIM)
        kc = slice(kcols[h], kcols[h] + HEAD_DIM)
        vc = slice(vcols[h], vcols[h] + HEAD_DIM)
        q = q_ref[:, cols]
        k_halo = jnp.concatenate([kp_ref[:, kc], kn_ref[:, kc]], axis=0)
        v_halo = jnp.concatenate([vp_ref[:, vc], vn_ref[:, vc]], axis=0)
        s_m = lax.dot_general(q, km_ref[:, kc], NT_DIMS, preferred_element_type=F32)
        s_h = lax.dot_general(q, k_halo, NT_DIMS, preferred_element_type=F32)
        s_m = jnp.where(ok_main, s_m, NEG_INF)
        s_h = jnp.where(ok_halo, s_h, NEG_INF)
        m = jnp.maximum(jnp.max(s_m, axis=-1, keepdims=True), jnp.max(s_h, axis=-1, keepdims=True))
        p_m = jnp.exp(s_m - m)
        p_h = jnp.exp(s_h - m)
        l = jnp.sum(p_m, axis=-1, keepdims=True) + jnp.sum(p_h, axis=-1, keepdims=True)
        acc = (jnp.dot(p_m.astype(BF16), vm_ref[:, vc], preferred_element_type=F32)
               + jnp.dot(p_h.astype(BF16), v_halo, preferred_element_type=F32))
        if with_sink:
            o = acc / (l + jnp.exp(sink_ref[h] - m))
            o_ref[:, cols] = (o * _silu(g_ref[:, cols].astype(F32))).astype(BF16)
        else:
            o_ref[:, cols] = acc / l
            lse_ref[:, h:h + 1] = m + jnp.log(l)


def _banded_specs(tq, half, q_sec, k_sec, v_sec, nblk):
    r = tq // half

    def main(sec):
        return pl.BlockSpec((None, None, tq, SEC), lambda c, lb, *_: (sec, c, lb, 0))

    def prev(sec):
        return pl.BlockSpec((None, None, half, SEC),
                            lambda c, lb, *_: (sec, c, jnp.maximum(lb * r - 1, 0), 0))

    def nxt(sec):
        return pl.BlockSpec((None, None, half, SEC),
                            lambda c, lb, *_: (sec, c, jnp.minimum((lb + 1) * r, nblk * r - 1), 0))

    return [main(q_sec), main(k_sec), prev(k_sec), nxt(k_sec),
            main(v_sec), prev(v_sec), nxt(v_sec)]


def _dilated_group(qkv, half, tq):
    _, dil, length, _ = qkv.shape
    tq = min(tq, length)
    nblk = length // tq
    cols = tuple(h * HEAD_DIM for h in range(N_HALF_HEADS))
    return pl.pallas_call(
        functools.partial(_banded_kernel, tq, half, cols, cols, False),
        grid=(dil, nblk),
        in_specs=_banded_specs(tq, half, 0, 1, 2, nblk),
        out_specs=[pl.BlockSpec((None, tq, SEC), lambda c, lb: (c, lb, 0)),
                   pl.BlockSpec((None, tq, LANES), lambda c, lb: (c, lb, 0))],
        out_shape=[jax.ShapeDtypeStruct((dil, length, SEC), F32),
                   jax.ShapeDtypeStruct((dil, length, LANES), F32)],
        compiler_params=pltpu.CompilerParams(
            dimension_semantics=("arbitrary", "arbitrary"),
            vmem_limit_bytes=VMEM_LIMIT_BYTES),
        name=f"dilated_x{dil}",
    )(*([qkv] * 7))


def _windowed_sink(proj, sink, tq):
    s = proj.shape[2]
    half = D_HALF_WINDOW
    nblk = s // tq
    grp = N_HALF_HEADS // D_KV_HEADS
    kcols = tuple((h // grp) * HEAD_DIM for h in range(N_HALF_HEADS))
    vcols = tuple(D_KV_HEADS * HEAD_DIM + (h // grp) * HEAD_DIM for h in range(N_HALF_HEADS))
    specs = _banded_specs(tq, half, 4, 5, 5, nblk)
    specs.append(pl.BlockSpec((None, None, tq, SEC), lambda c, lb, *_: (6, c, lb, 0)))
    grid_spec = pltpu.PrefetchScalarGridSpec(
        num_scalar_prefetch=1,
        grid=(1, nblk),
        in_specs=specs,
        out_specs=pl.BlockSpec((tq, SEC), lambda c, lb, *_: (lb, 0)),
    )
    return pl.pallas_call(
        functools.partial(_banded_kernel, tq, half, kcols, vcols, True),
        grid_spec=grid_spec,
        out_shape=jax.ShapeDtypeStruct((s, SEC), BF16),
        compiler_params=pltpu.CompilerParams(
            dimension_semantics=("arbitrary", "arbitrary"),
            vmem_limit_bytes=VMEM_LIMIT_BYTES),
        name="windowed_sink",
    )(sink.astype(F32), *([proj] * 8))


def _merge_kernel(dils, o0_ref, o1_ref, o2_ref, l0_ref, l1_ref, l2_ref, g_ref, out_ref,
                  *scratch):
    tm = out_ref.shape[0]
    o_refs = [o0_ref, o1_ref, o2_ref]
    l_refs = [l0_ref, l1_ref, l2_ref]
    head_out, l_tok = [], []
    for gi, dil in enumerate(dils):
        if dil == 1:
            head_out.append(lambda h, r=o_refs[gi]: r[0, :, h * HEAD_DIM:(h + 1) * HEAD_DIM])
            l_tok.append(l_refs[gi].at[0])
            continue
        ou_ref, lu_ref = scratch[2 * (gi - 1)], scratch[2 * (gi - 1) + 1]
        for c in range(dil):
            rows = pl.ds(c, tm // dil, stride=dil)
            for h in range(N_HALF_HEADS):
                ou_ref[h, rows, :] = o_refs[gi][c, :, h * HEAD_DIM:(h + 1) * HEAD_DIM]
            lu_ref[rows, :] = l_refs[gi][c]
        head_out.append(lambda h, r=ou_ref: r[h])
        l_tok.append(lu_ref)
    for h in range(N_HALF_HEADS):
        cols = slice(h * HEAD_DIM, (h + 1) * HEAD_DIM)
        ls = [r[:, h:h + 1] for r in l_tok]
        m = jnp.maximum(jnp.maximum(ls[0], ls[1]), ls[2])
        es = [jnp.exp(x - m) for x in ls]
        den = es[0] + es[1] + es[2]
        o = ((es[0] / den) * head_out[0](h) + (es[1] / den) * head_out[1](h)
             + (es[2] / den) * head_out[2](h))
        out_ref[:, cols] = (o * _silu(g_ref[:, cols].astype(F32))).astype(BF16)


def _merge_groups(outs, lses, proj, gate_sec, tm):
    s = proj.shape[1]
    dils = tuple(o.shape[0] for o in outs)
    assert dils[0] == 1
    in_specs = [pl.BlockSpec((d, tm // d, SEC), lambda i: (0, i, 0)) for d in dils]
    in_specs += [pl.BlockSpec((d, tm // d, LANES), lambda i: (0, i, 0)) for d in dils]
    in_specs.append(pl.BlockSpec((None, tm, SEC), lambda i: (gate_sec, i, 0)))
    scratch = []
    for d in dils[1:]:
        scratch += [pltpu.VMEM((N_HALF_HEADS, tm, HEAD_DIM), F32), pltpu.VMEM((tm, LANES), F32)]
    return pl.pallas_call(
        functools.partial(_merge_kernel, dils),
        grid=(s // tm,),
        in_specs=in_specs,
        out_specs=pl.BlockSpec((tm, SEC), lambda i: (i, 0)),
        out_shape=jax.ShapeDtypeStruct((s, SEC), BF16),
        scratch_shapes=scratch,
        compiler_params=pltpu.CompilerParams(
            dimension_semantics=("arbitrary",), vmem_limit_bytes=VMEM_LIMIT_BYTES),
        name="merge_groups",
    )(*outs, *lses, proj)


def _outproj_kernel(a_ref, b_ref, w_ref, g_ref, x_ref, o_ref):
    y = (jnp.dot(a_ref[...], w_ref[0:SEC, :], preferred_element_type=F32)
         + jnp.dot(b_ref[...], w_ref[SEC:2 * SEC, :], preferred_element_type=F32))
    ms = jnp.mean(y * y, axis=-1, keepdims=True)
    o_ref[...] = x_ref[...] + y * lax.rsqrt(ms + RMS_EPS) * g_ref[...]


def _outproj(mix_a, mix_b, w_out, post_g, x, tm):
    s = x.shape[0]
    half_spec = pl.BlockSpec((tm, SEC), lambda i: (i, 0))
    return pl.pallas_call(
        _outproj_kernel,
        grid=(s // tm,),
        in_specs=[half_spec, half_spec,
                  pl.BlockSpec((2 * SEC, D_MODEL), lambda i: (0, 0)),
                  pl.BlockSpec((1, D_MODEL), lambda i: (0, 0)),
                  pl.BlockSpec((tm, D_MODEL), lambda i: (i, 0))],
        out_specs=pl.BlockSpec((tm, D_MODEL), lambda i: (i, 0)),
        out_shape=jax.ShapeDtypeStruct((s, D_MODEL), F32),
        compiler_params=pltpu.CompilerParams(
            dimension_semantics=("arbitrary",), vmem_limit_bytes=VMEM_LIMIT_BYTES),
        name="outproj",
    )(mix_a, mix_b, w_out, post_g.reshape(1, D_MODEL), x)


def _tiles(s):
    return dict(proj_tm=min(1024, s), out_tm=min(512, s), diff_tq=min(512, s),
                diff_tk=min(1024, s // 2), band_tq=min(256, s), merge_tm=min(512, s))


def _even_layer(x, pre_g, w_in, rpb, lam_q1, lam_k1, lam_q2, lam_k2, subln_g, w_out, post_g):
    s = x.shape[0]
    t = _tiles(s)
    tables = _rope_tables(s, DIFF_QK_DIM)
    sections = [
        (ALL_PLAIN, HEAD_DIM ** -0.5), (ALL_PLAIN, 1.0), (ALL_PLAIN, 1.0), (ALL_PLAIN, 1.0),
        (ALL_ROPE, DIFF_QK_DIM ** -0.5 * LOG2_E), (ALL_ROPE, 1.0), (ALL_PLAIN, 1.0), (ALL_PLAIN, 1.0),
    ]
    w = jnp.concatenate([
        w_in[:, :4 * SEC],
        _permute_head_dims(w_in[:, 4 * SEC:5 * SEC], DIFF_QK_DIM),
        _permute_head_dims(w_in[:, 5 * SEC:6 * SEC], DIFF_QK_DIM),
        w_in[:, 6 * SEC:]], axis=1).astype(BF16)
    (proj,) = _inproj(x, pre_g, w, tables, sections, len(sections), (), t["proj_tm"])
    mix_a = _natten(proj, rpb)
    vt = proj[6].reshape(s, N_HALF_HEADS, HEAD_DIM).transpose(1, 2, 0)
    vt = jnp.concatenate([vt, jnp.ones((N_HALF_HEADS, DIFF_ONES_ROWS, s), BF16)], axis=1)
    lam_vecs = jnp.stack([lam_q1, lam_k1, lam_q2, lam_k2]).astype(F32)
    mix_b = _diff_attention(proj, vt, lam_vecs, subln_g, t["diff_tq"], t["diff_tk"])
    return _outproj(mix_a, mix_b, w_out.astype(BF16), post_g, x, t["out_tm"])


def _odd_layer(x, pre_g, w_in, sink, w_out, post_g):
    s = x.shape[0]
    t = _tiles(s)
    tables = _rope_tables(s, HEAD_DIM)
    scale = HEAD_DIM ** -0.5
    n_groups = len(C_PATTERNS)
    kv_w = D_KV_HEADS * HEAD_DIM
    qc, kc, vc = (w_in[:, i * n_groups * SEC:(i + 1) * n_groups * SEC] for i in range(3))
    rest = w_in[:, 3 * n_groups * SEC:]
    gc, qd, kd, vd, gd = (rest[:, :SEC], rest[:, SEC:2 * SEC], rest[:, 2 * SEC:2 * SEC + kv_w],
                          rest[:, 2 * SEC + kv_w:2 * SEC + 2 * kv_w], rest[:, 2 * SEC + 2 * kv_w:])

    def group(w3, gi):
        return w3[:, gi * SEC:(gi + 1) * SEC]

    def perm(wq):
        return _permute_head_dims(wq, HEAD_DIM)

    kvd = jnp.concatenate([perm(kd), vd, jnp.zeros((D_MODEL, SEC - 2 * kv_w), w_in.dtype)], axis=1)
    blocks = [perm(group(qc, 0)), perm(group(kc, 0)), group(vc, 0), gc, perm(qd), kvd, gd]
    kv_modes = (ROPE,) * D_KV_HEADS + (PLAIN,) * (CHUNKS - D_KV_HEADS)
    sections = [(ALL_ROPE, scale), (ALL_ROPE, 1.0), (ALL_PLAIN, 1.0), (ALL_PLAIN, 1.0),
                (ALL_ROPE, scale), (kv_modes, 1.0), (ALL_PLAIN, 1.0)]
    n_main = len(sections)
    fold_dils = tuple(dil for _, dil in C_PATTERNS[1:])
    for gi in range(1, n_groups):
        blocks += [perm(group(qc, gi)), perm(group(kc, gi)), group(vc, gi)]
        sections += [(ALL_ROPE, scale), (ALL_ROPE, 1.0), (ALL_PLAIN, 1.0)]
    w = jnp.concatenate(blocks, axis=1).astype(BF16)
    proj, *folds = _inproj(x, pre_g, w, tables, sections, n_main, fold_dils, t["proj_tm"])
    proj4 = proj.reshape(n_main, 1, s, SEC)
    outs, lses = [], []
    for gi, (window, dil) in enumerate(C_PATTERNS):
        o, lse = _dilated_group(proj4 if gi == 0 else folds[gi - 1], (window // 2) // dil,
                                t["band_tq"])
        outs.append(o)
        lses.append(lse)
    mix_c = _merge_groups(outs, lses, proj, 3, t["merge_tm"])
    mix_d = _windowed_sink(proj4, sink, t["band_tq"])
    return _outproj(mix_c, mix_d, w_out.astype(BF16), post_g, x, t["out_tm"])


def kernel(x, l0_pre_g, l0_w_in, l0_rpb, l0_lam_q1, l0_lam_k1, l0_lam_q2, l0_lam_k2, l0_subln_g,
           l0_w_out, l0_post_g, l1_pre_g, l1_w_in, l1_sink, l1_w_out, l1_post_g):
    b, s, d = x.shape
    assert b == 1 and d == D_MODEL
    y = x.reshape(s, d)
    y = _even_layer(y, l0_pre_g, l0_w_in, l0_rpb, l0_lam_q1, l0_lam_k1, l0_lam_q2, l0_lam_k2,
                    l0_subln_g, l0_w_out, l0_post_g)
    y = _odd_layer(y, l1_pre_g, l1_w_in, l1_sink, l1_w_out, l1_post_g)
    return y.reshape(b, s, d)
```

```python
import functools
import math

import numpy as np
import jax
import jax.numpy as jnp
from jax import lax
from jax.experimental import pallas as pl
from jax.experimental.pallas import tpu as pltpu

F32 = jnp.float32
BF16 = jnp.bfloat16

D_MODEL = 2048
HEAD_DIM = 128
N_HALF_HEADS = 8
SEC = N_HALF_HEADS * HEAD_DIM
GRID_W = 64
ROW_WIN = 8
COL_WIN = 16
DIFF_QK_DIM = 64
LAMBDA_INIT = 0.2
C_PATTERNS = ((128, 1), (512, 4), (2048, 16))
D_KV_HEADS = 2
D_HALF_WINDOW = 128
ROPE_THETA = 500000.0
ROPE_FRACTION = 4
RMS_EPS = 1e-6
NEG_INF = -1e30
LOG2_E = math.log2(math.e)

LANES = 128
MXU_N = 256
VMEM_LIMIT_BYTES = 56 * 1024 * 1024

PLAIN, ROPE = 0, 1
CHUNKS = SEC // LANES
ALL_PLAIN = (PLAIN,) * CHUNKS
ALL_ROPE = (ROPE,) * CHUNKS
NORM_ROWS = 256
ROPE_LANES = 16
DIFF_ONES_ROWS = 16
DIFF_UNROLL = 4
FOLD_SECTIONS = 3

NT_DIMS = (((1,), (1,)), ((), ()))


def _silu(g):
    return g / (1.0 + jnp.exp(-g))


SEC_PARAMS = 2 * CHUNKS + 1


def _inproj_kernel(n_main, fold_dils, par_ref, x_ref, g_ref, w_ref, cos_ref, sin_ref, *rest):
    n_fold = len(fold_dils)
    o_ref = rest[0]
    fold_refs = rest[1:1 + n_fold]
    h_ref = rest[1 + n_fold]
    acc_ref = rest[2 + n_fold] if n_fold else None
    j = pl.program_id(1)
    tm = x_ref.shape[0]

    @pl.when(j == 0)
    def _():
        rc = min(NORM_ROWS, tm)
        for r in range(tm // rc):
            x = x_ref[r * rc:(r + 1) * rc, :]
            ms = jnp.mean(x * x, axis=-1, keepdims=True)
            h_ref[r * rc:(r + 1) * rc, :] = (x * lax.rsqrt(ms + RMS_EPS) * g_ref[...]).astype(BF16)

    base = j * SEC_PARAMS
    scale = par_ref[base + 2 * CHUNKS]

    def epilogue(a, cos, sin, c):
        on = par_ref[base + c]
        off = par_ref[base + CHUNKS + c]
        a = a * (cos * on + off) + pltpu.roll(a, LANES // 2, 1) * (sin * on)
        return (a * scale).astype(BF16)

    def section(dest):
        for nt in range(SEC // MXU_N):
            acc = jnp.dot(h_ref[...], w_ref[:, nt * MXU_N:(nt + 1) * MXU_N],
                          preferred_element_type=F32)
            chunks = range(nt * MXU_N // LANES, (nt + 1) * MXU_N // LANES)
            if dest < 0:
                for c in chunks:
                    a = acc[:, (c * LANES) % MXU_N:(c * LANES) % MXU_N + LANES]
                    o_ref[:, c * LANES:(c + 1) * LANES] = epilogue(a, cos_ref[...], sin_ref[...], c)
            else:
                dil = fold_dils[dest]
                for c in chunks:
                    acc_ref[c] = acc[:, (c * LANES) % MXU_N:(c * LANES) % MXU_N + LANES]
                for r in range(dil):
                    rows = pl.ds(r, tm // dil, stride=dil)
                    for c in chunks:
                        fold_refs[dest][r, :, c * LANES:(c + 1) * LANES] = epilogue(
                            acc_ref[c, rows, :], cos_ref[rows, :], sin_ref[rows, :], c)

    if n_fold == 0:
        section(-1)
    else:
        pl.when(j < n_main)(functools.partial(section, -1))
        for f in range(n_fold):
            first = n_main + f * FOLD_SECTIONS
            pl.when((j >= first) & (j < first + FOLD_SECTIONS))(functools.partial(section, f))


def _inproj(x, pre_g, w, tables, sections, n_main, fold_dils, tm):
    s = x.shape[0]
    nsec = len(sections)
    assert nsec == n_main + FOLD_SECTIONS * len(fold_dils)
    cos_t, sin_t = tables
    params = []
    for modes, scale in sections:
        params += [float(m == ROPE) for m in modes] + [float(m != ROPE) for m in modes] + [scale]
    params = jnp.asarray(np.asarray(params, np.float32))
    tab_spec = pl.BlockSpec((tm, LANES), lambda i, j, p: (i, 0))
    out_specs = [pl.BlockSpec((None, tm, SEC), lambda i, j, p: (jnp.minimum(j, n_main - 1), i, 0))]
    out_shape = [jax.ShapeDtypeStruct((n_main, s, SEC), BF16)]
    scratch = [pltpu.VMEM((tm, D_MODEL), BF16)]
    for f, dil in enumerate(fold_dils):
        first = n_main + f * FOLD_SECTIONS
        out_specs.append(pl.BlockSpec(
            (None, dil, tm // dil, SEC),
            lambda i, j, p, first=first: (jnp.clip(j - first, 0, FOLD_SECTIONS - 1), 0, i, 0)))
        out_shape.append(jax.ShapeDtypeStruct((FOLD_SECTIONS, dil, s // dil, SEC), BF16))
    if fold_dils:
        scratch.append(pltpu.VMEM((CHUNKS, tm, LANES), F32))
    grid_spec = pltpu.PrefetchScalarGridSpec(
        num_scalar_prefetch=1,
        grid=(s // tm, nsec),
        in_specs=[
            pl.BlockSpec((tm, D_MODEL), lambda i, j, p: (i, 0)),
            pl.BlockSpec((1, D_MODEL), lambda i, j, p: (0, 0)),
            pl.BlockSpec((D_MODEL, SEC), lambda i, j, p: (0, j)),
            tab_spec, tab_spec,
        ],
        out_specs=out_specs,
        scratch_shapes=scratch,
    )
    return pl.pallas_call(
        functools.partial(_inproj_kernel, n_main, tuple(fold_dils)),
        grid_spec=grid_spec,
        out_shape=out_shape,
        compiler_params=pltpu.CompilerParams(
            dimension_semantics=("arbitrary", "arbitrary"),
            vmem_limit_bytes=VMEM_LIMIT_BYTES),
        name="inproj",
    )(params, x, pre_g.reshape(1, D_MODEL), w, cos_t, sin_t)


def _rope_tables(s, comp_dim):
    half = comp_dim // ROPE_FRACTION // 2
    n_comp = HEAD_DIM // comp_dim
    assert half * n_comp == ROPE_LANES
    inv = ROPE_THETA ** (-jnp.arange(half, dtype=F32) / half)
    ang = jnp.arange(s).astype(F32)[:, None] * inv[None, :]
    cos = jnp.tile(jnp.cos(ang), (1, n_comp))
    sin = jnp.tile(jnp.sin(ang), (1, n_comp))
    ones = jnp.ones((s, LANES // 2 - ROPE_LANES), F32)
    zeros = jnp.zeros((s, LANES // 2 - ROPE_LANES), F32)
    cos_t = jnp.concatenate([cos, ones, cos, ones], axis=1)
    sin_t = jnp.concatenate([-sin, zeros, sin, zeros], axis=1)
    return cos_t, sin_t


def _permute_head_dims(w, comp_dim):
    half = comp_dim // ROPE_FRACTION // 2
    d, width = w.shape
    nblk = HEAD_DIM // half
    w4 = w.reshape(d, width // HEAD_DIM, nblk, half)
    if comp_dim == HEAD_DIM:
        order = [(0, 1), (2, 5), (1, 2), (5, 8)]
    else:
        order = [(0, 1), (8, 9), (2, 5), (10, 13), (1, 2), (9, 10), (5, 8), (13, 16)]
    return jnp.concatenate([w4[:, :, a:b, :] for a, b in order], axis=2).reshape(d, width)


NA_QROWS = 4
NA_KROWS = 12
NA_TQ = NA_QROWS * GRID_W
NA_KBLK = 3


def _natten_kernel(q_ref, k0_ref, k1_ref, k2_ref, v0_ref, v1_ref, v2_ref, g_ref, bias_ref,
                   o_ref):
    k_refs = (k0_ref, k1_ref, k2_ref)
    v_refs = (v0_ref, v1_ref, v2_ref)
    heads = range(N_HALF_HEADS)
    blocks = range(NA_KBLK)
    scores = []
    for h in heads:
        cols = slice(h * HEAD_DIM, (h + 1) * HEAD_DIM)
        q = q_ref[:, cols]
        scores.append([
            lax.dot_general(q, k_refs[t][:, cols], NT_DIMS, preferred_element_type=F32)
            + bias_ref[h, :, t * NA_TQ:(t + 1) * NA_TQ] for t in blocks])
    probs = []
    for h in heads:
        m = functools.reduce(jnp.maximum,
                             [jnp.max(st, axis=-1, keepdims=True) for st in scores[h]])
        probs.append([jnp.exp2(st - m).astype(BF16) for st in scores[h]])
    ones = jnp.ones((NA_TQ, HEAD_DIM), BF16)
    for h in heads:
        cols = slice(h * HEAD_DIM, (h + 1) * HEAD_DIM)
        acc = functools.reduce(jnp.add, [
            jnp.dot(probs[h][t], jnp.concatenate([v_refs[t][:, cols], ones], axis=1),
                    preferred_element_type=F32) for t in blocks])
        o = acc[:, 0:HEAD_DIM] / acc[:, HEAD_DIM:2 * HEAD_DIM]
        o_ref[:, cols] = (o * _silu(g_ref[:, cols].astype(F32))).astype(BF16)


def _natten_bias_table(rpb, rows):
    n_heads, n_dr, n_dc = rpb.shape
    w = GRID_W
    wr = min(ROW_WIN, rows)
    nblk = rows // NA_QROWS
    lo = w - COL_WIN
    vec = jnp.pad(rpb, ((0, 0), (0, 0), (lo, 2 * w - lo - n_dc)))
    skew = jnp.tile(vec, (1, 1, w))[..., :w * (2 * w - 1)].reshape(n_heads, n_dr, w, 2 * w - 1)
    toep = skew[..., w - 1:2 * w - 1]
    qc = np.arange(w)[:, None]
    kc = np.arange(w)[None, :]
    cs = np.clip(qc - COL_WIN // 2, 0, w - COL_WIN)
    col_ok = (kc >= cs) & (kc < cs + COL_WIN)
    toep = jnp.where(jnp.asarray(col_ok), toep, NEG_INF)
    base = jnp.concatenate([toep, jnp.full((n_heads, 1, w, w), NEG_INF, F32)], axis=1)
    invalid = n_dr
    slots = []
    for blk in (0, 1, nblk - 1):
        r0 = blk * NA_QROWS
        ws = min(max(blk - 1, 0), nblk - NA_KBLK) * NA_QROWS
        for i in range(NA_QROWS):
            r = r0 + i
            rs = min(max(r - ROW_WIN // 2, 0), rows - wr)
            for jj in range(NA_KROWS):
                kr = ws + jj
                slots.append(kr - r + ROW_WIN - 1 if rs <= kr < rs + wr else invalid)
    tab = jnp.take(base, jnp.asarray(np.asarray(slots, np.int32)), axis=1)
    tab = tab.reshape(n_heads, 3, NA_QROWS, NA_KROWS, w, w).transpose(1, 0, 2, 4, 3, 5)
    return tab.reshape(3, n_heads, NA_TQ, NA_KBLK * NA_TQ)


def _natten(proj, rpb):
    s = proj.shape[1]
    rows = s // GRID_W
    nblk = rows // NA_QROWS
    bias = _natten_bias_table(rpb.astype(F32) * LOG2_E, rows)

    def kv_spec(sec, t):
        return pl.BlockSpec(
            (None, NA_TQ, SEC),
            lambda b: (sec, jnp.clip(b - 1, 0, nblk - NA_KBLK) + t, 0))

    def bias_idx(b):
        return (jnp.where(b == 0, 0, jnp.where(b == nblk - 1, 2, 1)), 0, 0, 0)

    return pl.pallas_call(
        _natten_kernel,
        grid=(nblk,),
        in_specs=[
            pl.BlockSpec((None, NA_TQ, SEC), lambda b: (0, b, 0)),
            kv_spec(1, 0), kv_spec(1, 1), kv_spec(1, 2),
            kv_spec(2, 0), kv_spec(2, 1), kv_spec(2, 2),
            pl.BlockSpec((None, NA_TQ, SEC), lambda b: (3, b, 0)),
            pl.BlockSpec((None, N_HALF_HEADS, NA_TQ, NA_KBLK * NA_TQ), bias_idx),
        ],
        out_specs=pl.BlockSpec((NA_TQ, SEC), lambda b: (b, 0)),
        out_shape=jax.ShapeDtypeStruct((s, SEC), BF16),
        compiler_params=pltpu.CompilerParams(
            dimension_semantics=("arbitrary",), vmem_limit_bytes=VMEM_LIMIT_BYTES),
        name="natten",
    )(proj, proj, proj, proj, proj, proj, proj, proj, bias)


def _diff_kernel(tq, tk, q_ref, qn_ref, k_ref, vt_ref, g_ref, lam_ref, subg_ref, o_ref,
                 qbd_ref, s_ref, m_ref, acc_ref):
    s = k_ref.shape[0]
    n_chunks = s // tk
    n_iters = n_chunks // DIFF_UNROLL
    qi = pl.program_id(1)
    cur = lax.rem(qi, 2)
    nxt = 1 - cur

    def build_qbd(slot, q):
        l64 = lax.broadcasted_iota(jnp.int32, (tq, HEAD_DIM), 1) % (LANES // 2)
        first = (l64 < ROPE_LANES // 2) | ((l64 >= ROPE_LANES) & (l64 < ROPE_LANES + 24))
        zero = jnp.zeros_like(q)
        qbd_ref[slot, 0:tq, :] = jnp.where(first, q, zero)
        qbd_ref[slot, tq:2 * tq, :] = jnp.where(first, zero, q)

    def scores(chunk, slot):
        start = pl.multiple_of(chunk * tk, tk)
        k = k_ref[pl.ds(start, tk), :]
        return lax.dot_general(k, qbd_ref[slot], NT_DIMS, preferred_element_type=F32)

    @pl.when(qi == 0)
    def _():
        build_qbd(0, q_ref[...])
        s_ref[0] = scores(0, 0)

    build_qbd(nxt, qn_ref[...])
    m_ref[...] = jnp.full(m_ref.shape, NEG_INF, F32)
    acc_ref[...] = jnp.zeros(acc_ref.shape, F32)

    def accumulate(slot, chunk):
        st = s_ref[slot]
        m_prev = m_ref[...]
        m_new = jnp.maximum(m_prev, jnp.max(st, axis=0, keepdims=True))
        alpha = jnp.exp2(m_prev - m_new)
        p = jnp.exp2(st - m_new).astype(BF16)
        start = pl.multiple_of(chunk * tk, tk)
        vt = vt_ref[:, pl.ds(start, tk)]
        acc_ref[...] = alpha * acc_ref[...] + jnp.dot(vt, p, preferred_element_type=F32)
        m_ref[...] = m_new

    def body(it, carry):
        c0 = DIFF_UNROLL * it
        last = it == n_iters - 1
        for u in range(DIFF_UNROLL):
            if u < DIFF_UNROLL - 1:
                s_ref[(u + 1) % 2] = scores(c0 + u + 1, cur)
            else:
                s_ref[(u + 1) % 2] = scores(jnp.where(last, 0, c0 + u + 1),
                                            jnp.where(last, nxt, cur))
            accumulate(u % 2, c0 + u)
        return carry

    lax.fori_loop(0, n_iters, body, 0)

    lam_v = lam_ref[...]
    lam = (jnp.exp(jnp.sum(lam_v[0:1] * lam_v[1:2], axis=-1, keepdims=True))
           - jnp.exp(jnp.sum(lam_v[2:3] * lam_v[3:4], axis=-1, keepdims=True))
           + LAMBDA_INIT)
    on = acc_ref[0:HEAD_DIM, :] / acc_ref[HEAD_DIM:HEAD_DIM + 1, :]
    ot = on[:, 0:tq] - lam * on[:, tq:2 * tq]
    ms = jnp.mean(ot * ot, axis=0, keepdims=True)
    ot = ot * lax.rsqrt(ms + RMS_EPS) * subg_ref[...] * (1.0 - LAMBDA_INIT)
    o = ot.T
    o_ref[...] = (o * _silu(g_ref[...].astype(F32))).astype(BF16)


def _diff_attention(proj, vt, lam_vecs, subln_g, tq, tk):
    s = proj.shape[1]
    vt_rows = vt.shape[1]
    assert (s // tk) % DIFF_UNROLL == 0 and DIFF_UNROLL % 2 == 0
    n_q = s // tq
    return pl.pallas_call(
        functools.partial(_diff_kernel, tq, tk),
        grid=(N_HALF_HEADS, n_q),
        in_specs=[
            pl.BlockSpec((None, tq, HEAD_DIM), lambda h, i: (4, i, h)),
            pl.BlockSpec((None, tq, HEAD_DIM), lambda h, i: (4, jnp.minimum(i + 1, n_q - 1), h)),
            pl.BlockSpec((None, s, HEAD_DIM), lambda h, i: (5, 0, h)),
            pl.BlockSpec((None, vt_rows, s), lambda h, i: (h, 0, 0)),
            pl.BlockSpec((None, tq, HEAD_DIM), lambda h, i: (7, i, h)),
            pl.BlockSpec((4, DIFF_QK_DIM), lambda h, i: (0, 0)),
            pl.BlockSpec((HEAD_DIM, 1), lambda h, i: (0, 0)),
        ],
        out_specs=pl.BlockSpec((tq, HEAD_DIM), lambda h, i: (i, h)),
        out_shape=jax.ShapeDtypeStruct((s, SEC), BF16),
        scratch_shapes=[
            pltpu.VMEM((2, 2 * tq, HEAD_DIM), BF16),
            pltpu.VMEM((2, tk, 2 * tq), F32),
            pltpu.VMEM((1, 2 * tq), F32),
            pltpu.VMEM((vt_rows, 2 * tq), F32),
        ],
        compiler_params=pltpu.CompilerParams(
            dimension_semantics=("arbitrary", "arbitrary"),
            vmem_limit_bytes=VMEM_LIMIT_BYTES),
        name="diff_attn",
    )(proj, proj, proj, vt, proj, lam_vecs, subln_g.reshape(HEAD_DIM, 1))


def _banded_kernel(tq, half, kcols, vcols, with_sink, *refs):
    if with_sink:
        (sink_ref, q_ref, km_ref, kp_ref, kn_ref, vm_ref, vp_ref, vn_ref, g_ref, o_ref) = refs
    else:
        (q_ref, km_ref, kp_ref, kn_ref, vm_ref, vp_ref, vn_ref, o_ref, lse_ref) = refs
    lb = pl.program_id(1)
    nblk = pl.num_programs(1)

    qi = lax.broadcasted_iota(jnp.int32, (tq, tq), 0)
    kj = lax.broadcasted_iota(jnp.int32, (tq, tq), 1)
    ok_main = jnp.abs(qi - kj) <= half
    qh = lax.broadcasted_iota(jnp.int32, (tq, 2 * half), 0)
    jh = lax.broadcasted_iota(jnp.int32, (tq, 2 * half), 1)
    prev_lo = jnp.where(lb > 0, 0, half)
    next_hi = jnp.where(lb < nblk - 1, 2 * half, half)
    ok_prev = (jh >= prev_lo) & (jh < half) & (qh <= jh)
    ok_next = (jh >= half) & (jh < next_hi) & (tq + (jh - half) - qh <= half)
    ok_halo = ok_prev | ok_next

    if not with_sink:
        lse_ref[...] = jnp.zeros(lse_ref.shape, F32)

    heads = range(N_HALF_HEADS)
    s_main, s_halo = [], []
    for h in heads:
        kc = slice(kcols[h], kcols[h] + HEAD_DIM)
        q = q_ref[:, h * HEAD_DIM:(h + 1) * HEAD_DIM]
        k_halo = jnp.concatenate([kp_ref[:, kc], kn_ref[:, kc]], axis=0)
        s_m = lax.dot_general(q, km_ref[:, kc], NT_DIMS, preferred_element_type=F32)
        s_h = lax.dot_general(q, k_halo, NT_DIMS, preferred_element_type=F32)
        s_main.append(jnp.where(ok_main, s_m, NEG_INF))
        s_halo.append(jnp.where(ok_halo, s_h, NEG_INF))
    maxes, p_main, p_halo = [], [], []
    for h in heads:
        m = jnp.maximum(jnp.max(s_main[h], axis=-1, keepdims=True),
                        jnp.max(s_halo[h], axis=-1, keepdims=True))
        maxes.append(m)
        p_main.append(jnp.exp2(s_main[h] - m).astype(BF16))
        p_halo.append(jnp.exp2(s_halo[h] - m).astype(BF16))
    ones_main = jnp.ones((tq, HEAD_DIM), BF16)
    ones_halo = jnp.ones((2 * half, HEAD_DIM), BF16)
    for h in heads:
        cols = slice(h * HEAD_DIM, (h + 1) * HEAD_DIM)
        vc = slice(vcols[h], vcols[h] + HEAD_DIM)
        v_main = jnp.concatenate([vm_ref[:, vc], ones_main], axis=1)
        v_halo = jnp.concatenate(
            [jnp.concatenate([vp_ref[:, vc], vn_ref[:, vc]], axis=0), ones_halo], axis=1)
        acc = (jnp.dot(p_main[h], v_main, preferred_element_type=F32)
               + jnp.dot(p_halo[h], v_halo, preferred_element_type=F32))
        num, den = acc[:, 0:HEAD_DIM], acc[:, HEAD_DIM:2 * HEAD_DIM]
        if with_sink:
            o = num / (den + jnp.exp2(sink_ref[h] - maxes[h]))
            o_ref[:, cols] = (o * _silu(g_ref[:, cols].astype(F32))).astype(BF16)
        else:
            o_ref[:, cols] = num / den
            lse_ref[:, h:h + 1] = maxes[h] + jnp.log2(den[:, 0:1])


def _banded_specs(tq, half, q_sec, k_sec, v_sec, nblk):
    r = tq // half

    def main(sec):
        return pl.BlockSpec((None, None, tq, SEC), lambda c, lb, *_: (sec, c, lb, 0))

    def prev(sec):
        return pl.BlockSpec((None, None, half, SEC),
                            lambda c, lb, *_: (sec, c, jnp.maximum(lb * r - 1, 0), 0))

    def nxt(sec):
        return pl.BlockSpec((None, None, half, SEC),
                            lambda c, lb, *_: (sec, c, jnp.minimum((lb + 1) * r, nblk * r - 1), 0))

    return [main(q_sec), main(k_sec), prev(k_sec), nxt(k_sec),
            main(v_sec), prev(v_sec), nxt(v_sec)]


def _dilated_group(qkv, half, tq):
    _, dil, length, _ = qkv.shape
    tq = min(tq, length)
    nblk = length // tq
    cols = tuple(h * HEAD_DIM for h in range(N_HALF_HEADS))
    return pl.pallas_call(
        functools.partial(_banded_kernel, tq, half, cols, cols, False),
        grid=(dil, nblk),
        in_specs=_banded_specs(tq, half, 0, 1, 2, nblk),
        out_specs=[pl.BlockSpec((None, tq, SEC), lambda c, lb: (c, lb, 0)),
                   pl.BlockSpec((None, tq, LANES), lambda c, lb: (c, lb, 0))],
        out_shape=[jax.ShapeDtypeStruct((dil, length, SEC), F32),
                   jax.ShapeDtypeStruct((dil, length, LANES), F32)],
        compiler_params=pltpu.CompilerParams(
            dimension_semantics=("arbitrary", "arbitrary"),
            vmem_limit_bytes=VMEM_LIMIT_BYTES),
        name=f"dilated_x{dil}",
    )(*([qkv] * 7))


def _windowed_sink(proj, sink, tq):
    s = proj.shape[2]
    half = D_HALF_WINDOW
    nblk = s // tq
    grp = N_HALF_HEADS // D_KV_HEADS
    kcols = tuple((h // grp) * HEAD_DIM for h in range(N_HALF_HEADS))
    vcols = tuple(D_KV_HEADS * HEAD_DIM + (h // grp) * HEAD_DIM for h in range(N_HALF_HEADS))
    specs = _banded_specs(tq, half, 4, 5, 5, nblk)
    specs.append(pl.BlockSpec((None, None, tq, SEC), lambda c, lb, *_: (6, c, lb, 0)))
    grid_spec = pltpu.PrefetchScalarGridSpec(
        num_scalar_prefetch=1,
        grid=(1, nblk),
        in_specs=specs,
        out_specs=pl.BlockSpec((tq, SEC), lambda c, lb, *_: (lb, 0)),
    )
    return pl.pallas_call(
        functools.partial(_banded_kernel, tq, half, kcols, vcols, True),
        grid_spec=grid_spec,
        out_shape=jax.ShapeDtypeStruct((s, SEC), BF16),
        compiler_params=pltpu.CompilerParams(
            dimension_semantics=("arbitrary", "arbitrary"),
            vmem_limit_bytes=VMEM_LIMIT_BYTES),
        name="windowed_sink",
    )(sink.astype(F32) * LOG2_E, *([proj] * 8))


def _merge_kernel(dils, o0_ref, o1_ref, o2_ref, l0_ref, l1_ref, l2_ref, g_ref, out_ref,
                  *scratch):
    tm = out_ref.shape[0]
    o_refs = [o0_ref, o1_ref, o2_ref]
    l_refs = [l0_ref, l1_ref, l2_ref]
    head_out, l_tok = [], []
    for gi, dil in enumerate(dils):
        if dil == 1:
            head_out.append(lambda h, r=o_refs[gi]: r[0, :, h * HEAD_DIM:(h + 1) * HEAD_DIM])
            l_tok.append(l_refs[gi].at[0])
            continue
        ou_ref, lu_ref = scratch[2 * (gi - 1)], scratch[2 * (gi - 1) + 1]
        for c in range(dil):
            rows = pl.ds(c, tm // dil, stride=dil)
            for h in range(N_HALF_HEADS):
                ou_ref[h, rows, :] = o_refs[gi][c, :, h * HEAD_DIM:(h + 1) * HEAD_DIM]
            lu_ref[rows, :] = l_refs[gi][c]
        head_out.append(lambda h, r=ou_ref: r[h])
        l_tok.append(lu_ref)
    for h in range(N_HALF_HEADS):
        cols = slice(h * HEAD_DIM, (h + 1) * HEAD_DIM)
        ls = [r[:, h:h + 1] for r in l_tok]
        m = jnp.maximum(jnp.maximum(ls[0], ls[1]), ls[2])
        es = [jnp.exp2(x - m) for x in ls]
        den = es[0] + es[1] + es[2]
        o = ((es[0] / den) * head_out[0](h) + (es[1] / den) * head_out[1](h)
             + (es[2] / den) * head_out[2](h))
        out_ref[:, cols] = (o * _silu(g_ref[:, cols].astype(F32))).astype(BF16)


def _merge_groups(outs, lses, proj, gate_sec, tm):
    s = proj.shape[1]
    dils = tuple(o.shape[0] for o in outs)
    assert dils[0] == 1
    in_specs = [pl.BlockSpec((d, tm // d, SEC), lambda i: (0, i, 0)) for d in dils]
    in_specs += [pl.BlockSpec((d, tm // d, LANES), lambda i: (0, i, 0)) for d in dils]
    in_specs.append(pl.BlockSpec((None, tm, SEC), lambda i: (gate_sec, i, 0)))
    scratch = []
    for d in dils[1:]:
        scratch += [pltpu.VMEM((N_HALF_HEADS, tm, HEAD_DIM), F32), pltpu.VMEM((tm, LANES), F32)]
    return pl.pallas_call(
        functools.partial(_merge_kernel, dils),
        grid=(s // tm,),
        in_specs=in_specs,
        out_specs=pl.BlockSpec((tm, SEC), lambda i: (i, 0)),
        out_shape=jax.ShapeDtypeStruct((s, SEC), BF16),
        scratch_shapes=scratch,
        compiler_params=pltpu.CompilerParams(
            dimension_semantics=("arbitrary",), vmem_limit_bytes=VMEM_LIMIT_BYTES),
        name="merge_groups",
    )(*outs, *lses, proj)


def _outproj_kernel(a_ref, b_ref, w_ref, g_ref, x_ref, o_ref):
    y = (jnp.dot(a_ref[...], w_ref[0:SEC, :], preferred_element_type=F32)
         + jnp.dot(b_ref[...], w_ref[SEC:2 * SEC, :], preferred_element_type=F32))
    ms = jnp.mean(y * y, axis=-1, keepdims=True)
    o_ref[...] = x_ref[...] + y * lax.rsqrt(ms + RMS_EPS) * g_ref[...]


def _outproj(mix_a, mix_b, w_out, post_g, x, tm):
    s = x.shape[0]
    half_spec = pl.BlockSpec((tm, SEC), lambda i: (i, 0))
    return pl.pallas_call(
        _outproj_kernel,
        grid=(s // tm,),
        in_specs=[half_spec, half_spec,
                  pl.BlockSpec((2 * SEC, D_MODEL), lambda i: (0, 0)),
                  pl.BlockSpec((1, D_MODEL), lambda i: (0, 0)),
                  pl.BlockSpec((tm, D_MODEL), lambda i: (i, 0))],
        out_specs=pl.BlockSpec((tm, D_MODEL), lambda i: (i, 0)),
        out_shape=jax.ShapeDtypeStruct((s, D_MODEL), F32),
        compiler_params=pltpu.CompilerParams(
            dimension_semantics=("arbitrary",), vmem_limit_bytes=VMEM_LIMIT_BYTES),
        name="outproj",
    )(mix_a, mix_b, w_out, post_g.reshape(1, D_MODEL), x)


def _tiles(s):
    return dict(proj_tm=min(1024, s), out_tm=min(512, s), diff_tq=min(512, s),
                diff_tk=min(1024, s // 2), band_tq=min(256, s), merge_tm=min(512, s))


def _even_layer(x, pre_g, w_in, rpb, lam_q1, lam_k1, lam_q2, lam_k2, subln_g, w_out, post_g):
    s = x.shape[0]
    t = _tiles(s)
    tables = _rope_tables(s, DIFF_QK_DIM)
    sections = [
        (ALL_PLAIN, HEAD_DIM ** -0.5 * LOG2_E), (ALL_PLAIN, 1.0), (ALL_PLAIN, 1.0), (ALL_PLAIN, 1.0),
        (ALL_ROPE, DIFF_QK_DIM ** -0.5 * LOG2_E), (ALL_ROPE, 1.0), (ALL_PLAIN, 1.0), (ALL_PLAIN, 1.0),
    ]
    w = jnp.concatenate([
        w_in[:, :4 * SEC],
        _permute_head_dims(w_in[:, 4 * SEC:5 * SEC], DIFF_QK_DIM),
        _permute_head_dims(w_in[:, 5 * SEC:6 * SEC], DIFF_QK_DIM),
        w_in[:, 6 * SEC:]], axis=1).astype(BF16)
    (proj,) = _inproj(x, pre_g, w, tables, sections, len(sections), (), t["proj_tm"])
    mix_a = _natten(proj, rpb)
    vt = proj[6].reshape(s, N_HALF_HEADS, HEAD_DIM).transpose(1, 2, 0)
    vt = jnp.concatenate([vt, jnp.ones((N_HALF_HEADS, DIFF_ONES_ROWS, s), BF16)], axis=1)
    lam_vecs = jnp.stack([lam_q1, lam_k1, lam_q2, lam_k2]).astype(F32)
    mix_b = _diff_attention(proj, vt, lam_vecs, subln_g, t["diff_tq"], t["diff_tk"])
    return _outproj(mix_a, mix_b, w_out.astype(BF16), post_g, x, t["out_tm"])


def _odd_layer(x, pre_g, w_in, sink, w_out, post_g):
    s = x.shape[0]
    t = _tiles(s)
    tables = _rope_tables(s, HEAD_DIM)
    scale = HEAD_DIM ** -0.5 * LOG2_E
    n_groups = len(C_PATTERNS)
    kv_w = D_KV_HEADS * HEAD_DIM
    qc, kc, vc = (w_in[:, i * n_groups * SEC:(i + 1) * n_groups * SEC] for i in range(3))
    rest = w_in[:, 3 * n_groups * SEC:]
    gc, qd, kd, vd, gd = (rest[:, :SEC], rest[:, SEC:2 * SEC], rest[:, 2 * SEC:2 * SEC + kv_w],
                          rest[:, 2 * SEC + kv_w:2 * SEC + 2 * kv_w], rest[:, 2 * SEC + 2 * kv_w:])

    def group(w3, gi):
        return w3[:, gi * SEC:(gi + 1) * SEC]

    def perm(wq):
        return _permute_head_dims(wq, HEAD_DIM)

    kvd = jnp.concatenate([perm(kd), vd, jnp.zeros((D_MODEL, SEC - 2 * kv_w), w_in.dtype)], axis=1)
    blocks = [perm(group(qc, 0)), perm(group(kc, 0)), group(vc, 0), gc, perm(qd), kvd, gd]
    kv_modes = (ROPE,) * D_KV_HEADS + (PLAIN,) * (CHUNKS - D_KV_HEADS)
    sections = [(ALL_ROPE, scale), (ALL_ROPE, 1.0), (ALL_PLAIN, 1.0), (ALL_PLAIN, 1.0),
                (ALL_ROPE, scale), (kv_modes, 1.0), (ALL_PLAIN, 1.0)]
    n_main = len(sections)
    fold_dils = tuple(dil for _, dil in C_PATTERNS[1:])
    for gi in range(1, n_groups):
        blocks += [perm(group(qc, gi)), perm(group(kc, gi)), group(vc, gi)]
        sections += [(ALL_ROPE, scale), (ALL_ROPE, 1.0), (ALL_PLAIN, 1.0)]
    w = jnp.concatenate(blocks, axis=1).astype(BF16)
    proj, *folds = _inproj(x, pre_g, w, tables, sections, n_main, fold_dils, t["proj_tm"])
    proj4 = proj.reshape(n_main, 1, s, SEC)
    outs, lses = [], []
    for gi, (window, dil) in enumerate(C_PATTERNS):
        o, lse = _dilated_group(proj4 if gi == 0 else folds[gi - 1], (window // 2) // dil,
                                t["band_tq"])
        outs.append(o)
        lses.append(lse)
    mix_c = _merge_groups(outs, lses, proj, 3, t["merge_tm"])
    mix_d = _windowed_sink(proj4, sink, t["band_tq"])
    return _outproj(mix_c, mix_d, w_out.astype(BF16), post_g, x, t["out_tm"])


def kernel(x, l0_pre_g, l0_w_in, l0_rpb, l0_lam_q1, l0_lam_k1, l0_lam_q2, l0_lam_k2, l0_subln_g,
           l0_w_out, l0_post_g, l1_pre_g, l1_w_in, l1_sink, l1_w_out, l1_post_g):
    b, s, d = x.shape
    assert b == 1 and d == D_MODEL
    y = x.reshape(s, d)
    y = _even_layer(y, l0_pre_g, l0_w_in, l0_rpb, l0_lam_q1, l0_lam_k1, l0_lam_q2, l0_lam_k2,
                    l0_subln_g, l0_w_out, l0_post_g)
    y = _odd_layer(y, l1_pre_g, l1_w_in, l1_sink, l1_w_out, l1_post_g)
    return y.reshape(b, s, d)
```

```python
import functools
import math

import numpy as np
import jax
import jax.numpy as jnp
from jax import lax
from jax.experimental import pallas as pl
from jax.experimental.pallas import tpu as pltpu

F32 = jnp.float32
BF16 = jnp.bfloat16

D_MODEL = 2048
HEAD_DIM = 128
N_HALF_HEADS = 8
SEC = N_HALF_HEADS * HEAD_DIM
GRID_W = 64
ROW_WIN = 8
COL_WIN = 16
DIFF_QK_DIM = 64
LAMBDA_INIT = 0.2
C_PATTERNS = ((128, 1), (512, 4), (2048, 16))
D_KV_HEADS = 2
D_HALF_WINDOW = 128
ROPE_THETA = 500000.0
ROPE_FRACTION = 4
RMS_EPS = 1e-6
NEG_INF = -1e30
LOG2_E = math.log2(math.e)

LANES = 128
MXU_N = 256
VMEM_LIMIT_BYTES = 56 * 1024 * 1024

PLAIN, ROPE = 0, 1
CHUNKS = SEC // LANES
ALL_PLAIN = (PLAIN,) * CHUNKS
ALL_ROPE = (ROPE,) * CHUNKS
NORM_ROWS = 256
MAIN_ROW_SPLIT = 4
DIFF_ONES_ROWS = 16
DIFF_UNROLL = 4
FOLD_SECTIONS = 3

NT_DIMS = (((1,), (1,)), ((), ()))


def _silu(g):
    return g / (1.0 + jnp.exp(-g))


SEC_PARAMS = 2 * CHUNKS + 1
FOLD, VT = "fold", "vt"


def _extra_sections(kind):
    return FOLD_SECTIONS if kind == FOLD else 1


def _inproj_kernel(n_main, extras, rope_half, par_ref, x_ref, g_ref, w_ref, cos_ref, sa_ref,
                   sb_ref, *rest):
    n_extra = len(extras)
    o_ref = rest[0]
    extra_refs = rest[1:1 + n_extra]
    h_ref = rest[1 + n_extra]
    acc_ref = rest[2 + n_extra] if any(kind == FOLD for kind, _ in extras) else None
    j = pl.program_id(1)
    tm = x_ref.shape[0]

    @pl.when(j == 0)
    def _():
        rc = min(NORM_ROWS, tm)
        for r in range(tm // rc):
            x = x_ref[r * rc:(r + 1) * rc, :]
            ms = jnp.mean(x * x, axis=-1, keepdims=True)
            h_ref[r * rc:(r + 1) * rc, :] = (x * lax.rsqrt(ms + RMS_EPS) * g_ref[...]).astype(BF16)

    base = j * SEC_PARAMS
    scale = par_ref[base + 2 * CHUNKS]

    def epilogue(a, c, rows=slice(None)):
        on = par_ref[base + c]
        off = par_ref[base + CHUNKS + c]
        a = (a * (cos_ref[rows, :] * on + off)
             + pltpu.roll(a, LANES - rope_half, 1) * (sa_ref[rows, :] * on)
             + pltpu.roll(a, rope_half, 1) * (sb_ref[rows, :] * on))
        return a * scale

    def section(dest):
        kind, dil = (None, None) if dest < 0 else extras[dest]
        if kind == VT:
            extra_refs[dest][:, HEAD_DIM:, :] = jnp.ones(
                (N_HALF_HEADS, DIFF_ONES_ROWS, tm), BF16)
        for nt in range(SEC // MXU_N):
            w_tile = w_ref[:, nt * MXU_N:(nt + 1) * MXU_N]
            chunks = range(nt * MXU_N // LANES, (nt + 1) * MXU_N // LANES)
            if kind is None:
                for rh in range(MAIN_ROW_SPLIT):
                    rows = slice(rh * tm // MAIN_ROW_SPLIT, (rh + 1) * tm // MAIN_ROW_SPLIT)
                    acc = jnp.dot(h_ref[rows, :], w_tile, preferred_element_type=F32)
                    for c in chunks:
                        a = acc[:, (c * LANES) % MXU_N:(c * LANES) % MXU_N + LANES]
                        o_ref[rows, c * LANES:(c + 1) * LANES] = epilogue(a, c, rows).astype(BF16)
                continue
            acc = jnp.dot(h_ref[...], w_tile, preferred_element_type=F32)
            for c in chunks:
                a = acc[:, (c * LANES) % MXU_N:(c * LANES) % MXU_N + LANES]
                if kind == VT:
                    extra_refs[dest][c, 0:HEAD_DIM, :] = a.T.astype(BF16)
                else:
                    acc_ref[c] = epilogue(a, c)
                    for r in range(dil):
                        rows = pl.ds(r, tm // dil, stride=dil)
                        extra_refs[dest][r, :, c * LANES:(c + 1) * LANES] = (
                            acc_ref[c, rows, :].astype(BF16))

    if n_extra == 0:
        section(-1)
    else:
        pl.when(j < n_main)(functools.partial(section, -1))
        first = n_main
        for e, (kind, _) in enumerate(extras):
            n = _extra_sections(kind)
            pl.when((j >= first) & (j < first + n))(functools.partial(section, e))
            first += n


def _inproj(x, pre_g, w, tables, rope_half, sections, n_main, extras, tm):
    s = x.shape[0]
    nsec = len(sections)
    assert nsec == n_main + sum(_extra_sections(kind) for kind, _ in extras)
    params = []
    for modes, scale in sections:
        params += [float(m == ROPE) for m in modes] + [float(m != ROPE) for m in modes] + [scale]
    params = jnp.asarray(np.asarray(params, np.float32))
    tab_spec = pl.BlockSpec((tm, LANES), lambda i, j, p: (i, 0))
    out_specs = [pl.BlockSpec((None, tm, SEC), lambda i, j, p: (jnp.minimum(j, n_main - 1), i, 0))]
    out_shape = [jax.ShapeDtypeStruct((n_main, s, SEC), BF16)]
    scratch = [pltpu.VMEM((tm, D_MODEL), BF16)]
    first = n_main
    for kind, dil in extras:
        if kind == FOLD:
            out_specs.append(pl.BlockSpec(
                (None, dil, tm // dil, SEC),
                lambda i, j, p, first=first: (jnp.clip(j - first, 0, FOLD_SECTIONS - 1), 0, i, 0)))
            out_shape.append(jax.ShapeDtypeStruct((FOLD_SECTIONS, dil, s // dil, SEC), BF16))
        else:
            rows = HEAD_DIM + DIFF_ONES_ROWS
            out_specs.append(pl.BlockSpec((N_HALF_HEADS, rows, tm), lambda i, j, p: (0, 0, i)))
            out_shape.append(jax.ShapeDtypeStruct((N_HALF_HEADS, rows, s), BF16))
        first += _extra_sections(kind)
    if any(kind == FOLD for kind, _ in extras):
        scratch.append(pltpu.VMEM((CHUNKS, tm, LANES), F32))
    grid_spec = pltpu.PrefetchScalarGridSpec(
        num_scalar_prefetch=1,
        grid=(s // tm, nsec),
        in_specs=[
            pl.BlockSpec((tm, D_MODEL), lambda i, j, p: (i, 0)),
            pl.BlockSpec((1, D_MODEL), lambda i, j, p: (0, 0)),
            pl.BlockSpec((D_MODEL, SEC), lambda i, j, p: (0, j)),
            tab_spec, tab_spec, tab_spec,
        ],
        out_specs=out_specs,
        scratch_shapes=scratch,
    )
    return pl.pallas_call(
        functools.partial(_inproj_kernel, n_main, tuple(extras), rope_half),
        grid_spec=grid_spec,
        out_shape=out_shape,
        compiler_params=pltpu.CompilerParams(
            dimension_semantics=("arbitrary", "arbitrary"),
            vmem_limit_bytes=VMEM_LIMIT_BYTES),
        name="inproj",
    )(params, x, pre_g.reshape(1, D_MODEL), w, *tables)


def _rope_tables(s, comp_dim):
    rd = comp_dim // ROPE_FRACTION
    half = rd // 2
    inv = ROPE_THETA ** (-jnp.arange(half, dtype=F32) / half)
    ang = jnp.arange(s).astype(F32)[:, None] * inv[None, :]
    cos, sin = jnp.cos(ang), jnp.sin(ang)
    ones = jnp.ones((s, comp_dim - rd), F32)
    zeros_h = jnp.zeros((s, half), F32)
    zeros_r = jnp.zeros((s, comp_dim - rd), F32)
    reps = LANES // comp_dim
    cos_t = jnp.tile(jnp.concatenate([cos, cos, ones], axis=1), (1, reps))
    sa_t = jnp.tile(jnp.concatenate([-sin, zeros_h, zeros_r], axis=1), (1, reps))
    sb_t = jnp.tile(jnp.concatenate([zeros_h, sin, zeros_r], axis=1), (1, reps))
    return (cos_t, sa_t, sb_t), half


NA_QROWS = 4
NA_KROWS = 12
NA_TQ = NA_QROWS * GRID_W
NA_KBLK = 3


def _natten_kernel(q_ref, k0_ref, k1_ref, k2_ref, v0_ref, v1_ref, v2_ref, g_ref, bias_ref,
                   o_ref):
    k_refs = (k0_ref, k1_ref, k2_ref)
    v_refs = (v0_ref, v1_ref, v2_ref)
    heads = range(N_HALF_HEADS)
    blocks = range(NA_KBLK)
    scores = []
    for h in heads:
        cols = slice(h * HEAD_DIM, (h + 1) * HEAD_DIM)
        q = q_ref[:, cols]
        scores.append([
            lax.dot_general(q, k_refs[t][:, cols], NT_DIMS, preferred_element_type=F32)
            + bias_ref[h, :, t * NA_TQ:(t + 1) * NA_TQ] for t in blocks])
    probs = []
    for h in heads:
        m = functools.reduce(jnp.maximum,
                             [jnp.max(st, axis=-1, keepdims=True) for st in scores[h]])
        probs.append([jnp.exp2(st - m).astype(BF16) for st in scores[h]])
    ones = jnp.ones((NA_TQ, HEAD_DIM), BF16)
    for h in heads:
        cols = slice(h * HEAD_DIM, (h + 1) * HEAD_DIM)
        acc = functools.reduce(jnp.add, [
            jnp.dot(probs[h][t], jnp.concatenate([v_refs[t][:, cols], ones], axis=1),
                    preferred_element_type=F32) for t in blocks])
        o = acc[:, 0:HEAD_DIM] / acc[:, HEAD_DIM:2 * HEAD_DIM]
        o_ref[:, cols] = (o * _silu(g_ref[:, cols].astype(F32))).astype(BF16)


def _natten_bias_table(rpb, rows):
    n_heads, n_dr, n_dc = rpb.shape
    w = GRID_W
    wr = min(ROW_WIN, rows)
    nblk = rows // NA_QROWS
    lo = w - COL_WIN
    vec = jnp.pad(rpb, ((0, 0), (0, 0), (lo, 2 * w - lo - n_dc)))
    skew = jnp.tile(vec, (1, 1, w))[..., :w * (2 * w - 1)].reshape(n_heads, n_dr, w, 2 * w - 1)
    toep = skew[..., w - 1:2 * w - 1]
    qc = np.arange(w)[:, None]
    kc = np.arange(w)[None, :]
    cs = np.clip(qc - COL_WIN // 2, 0, w - COL_WIN)
    col_ok = (kc >= cs) & (kc < cs + COL_WIN)
    toep = jnp.where(jnp.asarray(col_ok), toep, NEG_INF)
    base = jnp.concatenate([toep, jnp.full((n_heads, 1, w, w), NEG_INF, F32)], axis=1)
    invalid = n_dr
    slots = []
    for blk in (0, 1, nblk - 1):
        r0 = blk * NA_QROWS
        ws = min(max(blk - 1, 0), nblk - NA_KBLK) * NA_QROWS
        for i in range(NA_QROWS):
            r = r0 + i
            rs = min(max(r - ROW_WIN // 2, 0), rows - wr)
            for jj in range(NA_KROWS):
                kr = ws + jj
                slots.append(kr - r + ROW_WIN - 1 if rs <= kr < rs + wr else invalid)
    tab = jnp.take(base, jnp.asarray(np.asarray(slots, np.int32)), axis=1)
    tab = tab.reshape(n_heads, 3, NA_QROWS, NA_KROWS, w, w).transpose(1, 0, 2, 4, 3, 5)
    return tab.reshape(3, n_heads, NA_TQ, NA_KBLK * NA_TQ)


def _natten(proj, rpb):
    s = proj.shape[1]
    rows = s // GRID_W
    nblk = rows // NA_QROWS
    bias = _natten_bias_table(rpb.astype(F32) * LOG2_E, rows)

    def kv_spec(sec, t):
        return pl.BlockSpec(
            (None, NA_TQ, SEC),
            lambda b: (sec, jnp.clip(b - 1, 0, nblk - NA_KBLK) + t, 0))

    def bias_idx(b):
        return (jnp.where(b == 0, 0, jnp.where(b == nblk - 1, 2, 1)), 0, 0, 0)

    return pl.pallas_call(
        _natten_kernel,
        grid=(nblk,),
        in_specs=[
            pl.BlockSpec((None, NA_TQ, SEC), lambda b: (0, b, 0)),
            kv_spec(1, 0), kv_spec(1, 1), kv_spec(1, 2),
            kv_spec(2, 0), kv_spec(2, 1), kv_spec(2, 2),
            pl.BlockSpec((None, NA_TQ, SEC), lambda b: (3, b, 0)),
            pl.BlockSpec((None, N_HALF_HEADS, NA_TQ, NA_KBLK * NA_TQ), bias_idx),
        ],
        out_specs=pl.BlockSpec((NA_TQ, SEC), lambda b: (b, 0)),
        out_shape=jax.ShapeDtypeStruct((s, SEC), BF16),
        compiler_params=pltpu.CompilerParams(
            dimension_semantics=("arbitrary",), vmem_limit_bytes=VMEM_LIMIT_BYTES),
        name="natten",
    )(proj, proj, proj, proj, proj, proj, proj, proj, bias)


def _diff_kernel(tq, tk, q_ref, qn_ref, k_ref, vt_ref, g_ref, lam_ref, subg_ref, o_ref,
                 qbd_ref, s_ref, m_ref, acc_ref):
    s = k_ref.shape[0]
    n_chunks = s // tk
    n_iters = n_chunks // DIFF_UNROLL
    qi = pl.program_id(1)
    cur = lax.rem(qi, 2)
    nxt = 1 - cur

    def build_qbd(slot, q):
        first = lax.broadcasted_iota(jnp.int32, (tq, HEAD_DIM), 1) < DIFF_QK_DIM
        zero = jnp.zeros_like(q)
        qbd_ref[slot, 0:tq, :] = jnp.where(first, q, zero)
        qbd_ref[slot, tq:2 * tq, :] = jnp.where(first, zero, q)

    def scores(chunk, slot):
        start = pl.multiple_of(chunk * tk, tk)
        k = k_ref[pl.ds(start, tk), :]
        return lax.dot_general(k, qbd_ref[slot], NT_DIMS, preferred_element_type=F32)

    @pl.when(qi == 0)
    def _():
        build_qbd(0, q_ref[...])
        s_ref[0] = scores(0, 0)

    build_qbd(nxt, qn_ref[...])
    m_ref[...] = jnp.full(m_ref.shape, NEG_INF, F32)
    acc_ref[...] = jnp.zeros(acc_ref.shape, F32)

    def accumulate(slot, chunk):
        st = s_ref[slot]
        m_prev = m_ref[...]
        m_new = jnp.maximum(m_prev, jnp.max(st, axis=0, keepdims=True))
        alpha = jnp.exp2(m_prev - m_new)
        p = jnp.exp2(st - m_new).astype(BF16)
        start = pl.multiple_of(chunk * tk, tk)
        vt = vt_ref[:, pl.ds(start, tk)]
        acc_ref[...] = alpha * acc_ref[...] + jnp.dot(vt, p, preferred_element_type=F32)
        m_ref[...] = m_new

    def body(it, carry):
        c0 = DIFF_UNROLL * it
        last = it == n_iters - 1
        for u in range(DIFF_UNROLL):
            if u < DIFF_UNROLL - 1:
                s_ref[(u + 1) % 2] = scores(c0 + u + 1, cur)
            else:
                s_ref[(u + 1) % 2] = scores(jnp.where(last, 0, c0 + u + 1),
                                            jnp.where(last, nxt, cur))
            accumulate(u % 2, c0 + u)
        return carry

    lax.fori_loop(0, n_iters, body, 0)

    lam_v = lam_ref[...]
    lam = (jnp.exp(jnp.sum(lam_v[0:1] * lam_v[1:2], axis=-1, keepdims=True))
           - jnp.exp(jnp.sum(lam_v[2:3] * lam_v[3:4], axis=-1, keepdims=True))
           + LAMBDA_INIT)
    on = acc_ref[0:HEAD_DIM, :] / acc_ref[HEAD_DIM:HEAD_DIM + 1, :]
    ot = on[:, 0:tq] - lam * on[:, tq:2 * tq]
    ms = jnp.mean(ot * ot, axis=0, keepdims=True)
    ot = ot * lax.rsqrt(ms + RMS_EPS) * subg_ref[...] * (1.0 - LAMBDA_INIT)
    o = ot.T
    o_ref[...] = (o * _silu(g_ref[...].astype(F32))).astype(BF16)


def _diff_attention(proj, vt, lam_vecs, subln_g, tq, tk):
    s = proj.shape[1]
    vt_rows = vt.shape[1]
    assert (s // tk) % DIFF_UNROLL == 0 and DIFF_UNROLL % 2 == 0
    n_q = s // tq
    return pl.pallas_call(
        functools.partial(_diff_kernel, tq, tk),
        grid=(N_HALF_HEADS, n_q),
        in_specs=[
            pl.BlockSpec((None, tq, HEAD_DIM), lambda h, i: (4, i, h)),
            pl.BlockSpec((None, tq, HEAD_DIM), lambda h, i: (4, jnp.minimum(i + 1, n_q - 1), h)),
            pl.BlockSpec((None, s, HEAD_DIM), lambda h, i: (5, 0, h)),
            pl.BlockSpec((None, vt_rows, s), lambda h, i: (h, 0, 0)),
            pl.BlockSpec((None, tq, HEAD_DIM), lambda h, i: (6, i, h)),
            pl.BlockSpec((4, DIFF_QK_DIM), lambda h, i: (0, 0)),
            pl.BlockSpec((HEAD_DIM, 1), lambda h, i: (0, 0)),
        ],
        out_specs=pl.BlockSpec((tq, HEAD_DIM), lambda h, i: (i, h)),
        out_shape=jax.ShapeDtypeStruct((s, SEC), BF16),
        scratch_shapes=[
            pltpu.VMEM((2, 2 * tq, HEAD_DIM), BF16),
            pltpu.VMEM((2, tk, 2 * tq), F32),
            pltpu.VMEM((1, 2 * tq), F32),
            pltpu.VMEM((vt_rows, 2 * tq), F32),
        ],
        compiler_params=pltpu.CompilerParams(
            dimension_semantics=("arbitrary", "arbitrary"),
            vmem_limit_bytes=VMEM_LIMIT_BYTES),
        name="diff_attn",
    )(proj, proj, proj, vt, proj, lam_vecs, subln_g.reshape(HEAD_DIM, 1))


def _banded_kernel(tq, half, kcols, vcols, with_sink, *refs):
    if with_sink:
        (sink_ref, q_ref, km_ref, kp_ref, kn_ref, vm_ref, vp_ref, vn_ref, g_ref, o_ref) = refs
    else:
        (q_ref, km_ref, kp_ref, kn_ref, vm_ref, vp_ref, vn_ref, o_ref, lse_ref) = refs
    lb = pl.program_id(1)
    nblk = pl.num_programs(1)

    qi = lax.broadcasted_iota(jnp.int32, (tq, tq), 0)
    kj = lax.broadcasted_iota(jnp.int32, (tq, tq), 1)
    ok_main = jnp.abs(qi - kj) <= half
    qh = lax.broadcasted_iota(jnp.int32, (tq, 2 * half), 0)
    jh = lax.broadcasted_iota(jnp.int32, (tq, 2 * half), 1)
    prev_lo = jnp.where(lb > 0, 0, half)
    next_hi = jnp.where(lb < nblk - 1, 2 * half, half)
    ok_prev = (jh >= prev_lo) & (jh < half) & (qh <= jh)
    ok_next = (jh >= half) & (jh < next_hi) & (tq + (jh - half) - qh <= half)
    ok_halo = ok_prev | ok_next

    if not with_sink:
        lse_ref[...] = jnp.zeros(lse_ref.shape, F32)

    heads = range(N_HALF_HEADS)
    s_main, s_halo = [], []
    for h in heads:
        kc = slice(kcols[h], kcols[h] + HEAD_DIM)
        q = q_ref[:, h * HEAD_DIM:(h + 1) * HEAD_DIM]
        k_halo = jnp.concatenate([kp_ref[:, kc], kn_ref[:, kc]], axis=0)
        s_m = lax.dot_general(q, km_ref[:, kc], NT_DIMS, preferred_element_type=F32)
        s_h = lax.dot_general(q, k_halo, NT_DIMS, preferred_element_type=F32)
        s_main.append(jnp.where(ok_main, s_m, NEG_INF))
        s_halo.append(jnp.where(ok_halo, s_h, NEG_INF))
    maxes, p_main, p_halo = [], [], []
    for h in heads:
        m = jnp.maximum(jnp.max(s_main[h], axis=-1, keepdims=True),
                        jnp.max(s_halo[h], axis=-1, keepdims=True))
        maxes.append(m)
        p_main.append(jnp.exp2(s_main[h] - m).astype(BF16))
        p_halo.append(jnp.exp2(s_halo[h] - m).astype(BF16))
    ones_main = jnp.ones((tq, HEAD_DIM), BF16)
    ones_halo = jnp.ones((2 * half, HEAD_DIM), BF16)
    for h in heads:
        cols = slice(h * HEAD_DIM, (h + 1) * HEAD_DIM)
        vc = slice(vcols[h], vcols[h] + HEAD_DIM)
        v_main = jnp.concatenate([vm_ref[:, vc], ones_main], axis=1)
        v_halo = jnp.concatenate(
            [jnp.concatenate([vp_ref[:, vc], vn_ref[:, vc]], axis=0), ones_halo], axis=1)
        acc = (jnp.dot(p_main[h], v_main, preferred_element_type=F32)
               + jnp.dot(p_halo[h], v_halo, preferred_element_type=F32))
        num, den = acc[:, 0:HEAD_DIM], acc[:, HEAD_DIM:2 * HEAD_DIM]
        if with_sink:
            o = num / (den + jnp.exp2(sink_ref[h] - maxes[h]))
            o_ref[:, cols] = (o * _silu(g_ref[:, cols].astype(F32))).astype(BF16)
        else:
            o_ref[:, cols] = num / den
            lse_ref[:, h:h + 1] = maxes[h] + jnp.log2(den[:, 0:1])


def _banded_specs(tq, half, q_sec, k_sec, v_sec, nblk):
    r = tq // half

    def main(sec):
        return pl.BlockSpec((None, None, tq, SEC), lambda c, lb, *_: (sec, c, lb, 0))

    def prev(sec):
        return pl.BlockSpec((None, None, half, SEC),
                            lambda c, lb, *_: (sec, c, jnp.maximum(lb * r - 1, 0), 0))

    def nxt(sec):
        return pl.BlockSpec((None, None, half, SEC),
                            lambda c, lb, *_: (sec, c, jnp.minimum((lb + 1) * r, nblk * r - 1), 0))

    return [main(q_sec), main(k_sec), prev(k_sec), nxt(k_sec),
            main(v_sec), prev(v_sec), nxt(v_sec)]


def _dilated_group(qkv, half, tq):
    _, dil, length, _ = qkv.shape
    tq = min(tq, length)
    nblk = length // tq
    cols = tuple(h * HEAD_DIM for h in range(N_HALF_HEADS))
    return pl.pallas_call(
        functools.partial(_banded_kernel, tq, half, cols, cols, False),
        grid=(dil, nblk),
        in_specs=_banded_specs(tq, half, 0, 1, 2, nblk),
        out_specs=[pl.BlockSpec((None, tq, SEC), lambda c, lb: (c, lb, 0)),
                   pl.BlockSpec((None, tq, LANES), lambda c, lb: (c, lb, 0))],
        out_shape=[jax.ShapeDtypeStruct((dil, length, SEC), F32),
                   jax.ShapeDtypeStruct((dil, length, LANES), F32)],
        compiler_params=pltpu.CompilerParams(
            dimension_semantics=("arbitrary", "arbitrary"),
            vmem_limit_bytes=VMEM_LIMIT_BYTES),
        name=f"dilated_x{dil}",
    )(*([qkv] * 7))


def _windowed_sink(proj, sink, tq):
    s = proj.shape[2]
    half = D_HALF_WINDOW
    nblk = s // tq
    grp = N_HALF_HEADS // D_KV_HEADS
    kcols = tuple((h // grp) * HEAD_DIM for h in range(N_HALF_HEADS))
    vcols = tuple(D_KV_HEADS * HEAD_DIM + (h // grp) * HEAD_DIM for h in range(N_HALF_HEADS))
    specs = _banded_specs(tq, half, 4, 5, 5, nblk)
    specs.append(pl.BlockSpec((None, None, tq, SEC), lambda c, lb, *_: (6, c, lb, 0)))
    grid_spec = pltpu.PrefetchScalarGridSpec(
        num_scalar_prefetch=1,
        grid=(1, nblk),
        in_specs=specs,
        out_specs=pl.BlockSpec((tq, SEC), lambda c, lb, *_: (lb, 0)),
    )
    return pl.pallas_call(
        functools.partial(_banded_kernel, tq, half, kcols, vcols, True),
        grid_spec=grid_spec,
        out_shape=jax.ShapeDtypeStruct((s, SEC), BF16),
        compiler_params=pltpu.CompilerParams(
            dimension_semantics=("arbitrary", "arbitrary"),
            vmem_limit_bytes=VMEM_LIMIT_BYTES),
        name="windowed_sink",
    )(sink.astype(F32) * LOG2_E, *([proj] * 8))


def _merge_into(out_ref, dils, o_refs, l_refs, g_ref, scratch):
    tm = out_ref.shape[0]
    head_out, l_tok = [], []
    for gi, dil in enumerate(dils):
        if dil == 1:
            head_out.append(lambda h, r=o_refs[gi]: r[0, :, h * HEAD_DIM:(h + 1) * HEAD_DIM])
            l_tok.append(l_refs[gi].at[0])
            continue
        ou_ref, lu_ref = scratch[2 * (gi - 1)], scratch[2 * (gi - 1) + 1]
        for c in range(dil):
            rows = pl.ds(c, tm // dil, stride=dil)
            for h in range(N_HALF_HEADS):
                ou_ref[h, rows, :] = o_refs[gi][c, :, h * HEAD_DIM:(h + 1) * HEAD_DIM]
            lu_ref[rows, :] = l_refs[gi][c]
        head_out.append(lambda h, r=ou_ref: r[h])
        l_tok.append(lu_ref)
    for h in range(N_HALF_HEADS):
        cols = slice(h * HEAD_DIM, (h + 1) * HEAD_DIM)
        ls = [r[:, h:h + 1] for r in l_tok]
        m = jnp.maximum(jnp.maximum(ls[0], ls[1]), ls[2])
        es = [jnp.exp2(x - m) for x in ls]
        den = es[0] + es[1] + es[2]
        o = ((es[0] / den) * head_out[0](h) + (es[1] / den) * head_out[1](h)
             + (es[2] / den) * head_out[2](h))
        out_ref[:, cols] = (o * _silu(g_ref[:, cols].astype(F32))).astype(BF16)


def _project_out(a, b_ref, w_ref, g_ref, x_ref, o_ref):
    y = (jnp.dot(a, w_ref[0:SEC, :], preferred_element_type=F32)
         + jnp.dot(b_ref[...], w_ref[SEC:2 * SEC, :], preferred_element_type=F32))
    ms = jnp.mean(y * y, axis=-1, keepdims=True)
    o_ref[...] = x_ref[...] + y * lax.rsqrt(ms + RMS_EPS) * g_ref[...]


def _outproj_kernel(a_ref, b_ref, w_ref, g_ref, x_ref, o_ref):
    _project_out(a_ref[...], b_ref, w_ref, g_ref, x_ref, o_ref)


def _outproj_merge_kernel(dils, *refs):
    n = len(dils)
    o_refs, l_refs = refs[0:n], refs[n:2 * n]
    gate_ref, b_ref, w_ref, g_ref, x_ref, o_ref, mix_ref = refs[2 * n:2 * n + 7]
    _merge_into(mix_ref, dils, o_refs, l_refs, gate_ref, refs[2 * n + 7:])
    _project_out(mix_ref[...], b_ref, w_ref, g_ref, x_ref, o_ref)


def _outproj(mix_a, mix_b, w_out, post_g, x, tm):
    s = x.shape[0]
    half_spec = pl.BlockSpec((tm, SEC), lambda i: (i, 0))
    tail_specs = [half_spec,
                  pl.BlockSpec((2 * SEC, D_MODEL), lambda i: (0, 0), pipeline_mode=pl.Buffered(1)),
                  pl.BlockSpec((1, D_MODEL), lambda i: (0, 0)),
                  pl.BlockSpec((tm, D_MODEL), lambda i: (i, 0))]
    tail_args = (mix_b, w_out, post_g.reshape(1, D_MODEL), x)
    if isinstance(mix_a, tuple):
        outs, lses, proj, gate_sec = mix_a
        dils = tuple(o.shape[0] for o in outs)
        assert dils[0] == 1
        in_specs = [pl.BlockSpec((d, tm // d, SEC), lambda i: (0, i, 0)) for d in dils]
        in_specs += [pl.BlockSpec((d, tm // d, LANES), lambda i: (0, i, 0)) for d in dils]
        in_specs.append(pl.BlockSpec((None, tm, SEC), lambda i: (gate_sec, i, 0)))
        scratch = [pltpu.VMEM((tm, SEC), BF16)]
        for d in dils[1:]:
            scratch += [pltpu.VMEM((N_HALF_HEADS, tm, HEAD_DIM), F32), pltpu.VMEM((tm, LANES), F32)]
        body = functools.partial(_outproj_merge_kernel, dils)
        args = (*outs, *lses, proj)
    else:
        in_specs, scratch, body, args = [half_spec], [], _outproj_kernel, (mix_a,)
    return pl.pallas_call(
        body,
        grid=(s // tm,),
        in_specs=in_specs + tail_specs,
        out_specs=pl.BlockSpec((tm, D_MODEL), lambda i: (i, 0)),
        out_shape=jax.ShapeDtypeStruct((s, D_MODEL), F32),
        scratch_shapes=scratch,
        compiler_params=pltpu.CompilerParams(
            dimension_semantics=("arbitrary",), vmem_limit_bytes=VMEM_LIMIT_BYTES),
        name="outproj",
    )(*args, *tail_args)


def _tiles(s):
    return dict(proj_tm=min(1024, s), out_tm=min(512, s), diff_tq=min(512, s),
                diff_tk=min(1024, s // 4), band_tq=min(256, s))


def _even_layer(x, pre_g, w_in, rpb, lam_q1, lam_k1, lam_q2, lam_k2, subln_g, w_out, post_g):
    s = x.shape[0]
    t = _tiles(s)
    tables, rope_half = _rope_tables(s, DIFF_QK_DIM)
    sections = [
        (ALL_PLAIN, HEAD_DIM ** -0.5 * LOG2_E), (ALL_PLAIN, 1.0), (ALL_PLAIN, 1.0), (ALL_PLAIN, 1.0),
        (ALL_ROPE, DIFF_QK_DIM ** -0.5 * LOG2_E), (ALL_ROPE, 1.0), (ALL_PLAIN, 1.0),
        (ALL_PLAIN, 1.0),
    ]
    n_main = len(sections) - 1
    w = jnp.concatenate([w_in[:, :6 * SEC], w_in[:, 7 * SEC:], w_in[:, 6 * SEC:7 * SEC]],
                        axis=1).astype(BF16)
    proj, vt = _inproj(x, pre_g, w, tables, rope_half, sections, n_main, ((VT, None),),
                       t["proj_tm"])
    mix_a = _natten(proj, rpb)
    lam_vecs = jnp.stack([lam_q1, lam_k1, lam_q2, lam_k2]).astype(F32)
    mix_b = _diff_attention(proj, vt, lam_vecs, subln_g, t["diff_tq"], t["diff_tk"])
    return _outproj(mix_a, mix_b, w_out.astype(BF16), post_g, x, t["out_tm"])


def _odd_layer(x, pre_g, w_in, sink, w_out, post_g):
    s = x.shape[0]
    t = _tiles(s)
    tables, rope_half = _rope_tables(s, HEAD_DIM)
    scale = HEAD_DIM ** -0.5 * LOG2_E
    n_groups = len(C_PATTERNS)
    kv_w = D_KV_HEADS * HEAD_DIM
    qc, kc, vc = (w_in[:, i * n_groups * SEC:(i + 1) * n_groups * SEC] for i in range(3))
    rest = w_in[:, 3 * n_groups * SEC:]
    gc, qd, kd, vd, gd = (rest[:, :SEC], rest[:, SEC:2 * SEC], rest[:, 2 * SEC:2 * SEC + kv_w],
                          rest[:, 2 * SEC + kv_w:2 * SEC + 2 * kv_w], rest[:, 2 * SEC + 2 * kv_w:])

    def group(w3, gi):
        return w3[:, gi * SEC:(gi + 1) * SEC]

    kvd = jnp.concatenate([kd, vd, jnp.zeros((D_MODEL, SEC - 2 * kv_w), w_in.dtype)], axis=1)
    blocks = [group(qc, 0), group(kc, 0), group(vc, 0), gc, qd, kvd, gd]
    kv_modes = (ROPE,) * D_KV_HEADS + (PLAIN,) * (CHUNKS - D_KV_HEADS)
    sections = [(ALL_ROPE, scale), (ALL_ROPE, 1.0), (ALL_PLAIN, 1.0), (ALL_PLAIN, 1.0),
                (ALL_ROPE, scale), (kv_modes, 1.0), (ALL_PLAIN, 1.0)]
    n_main = len(sections)
    extras = tuple((FOLD, dil) for _, dil in C_PATTERNS[1:])
    for gi in range(1, n_groups):
        blocks += [group(qc, gi), group(kc, gi), group(vc, gi)]
        sections += [(ALL_ROPE, scale), (ALL_ROPE, 1.0), (ALL_PLAIN, 1.0)]
    w = jnp.concatenate(blocks, axis=1).astype(BF16)
    proj, *folds = _inproj(x, pre_g, w, tables, rope_half, sections, n_main, extras, t["proj_tm"])
    proj4 = proj.reshape(n_main, 1, s, SEC)
    outs, lses = [], []
    for gi, (window, dil) in enumerate(C_PATTERNS):
        o, lse = _dilated_group(proj4 if gi == 0 else folds[gi - 1], (window // 2) // dil,
                                t["band_tq"])
        outs.append(o)
        lses.append(lse)
    mix_d = _windowed_sink(proj4, sink, t["band_tq"])
    return _outproj((outs, lses, proj, 3), mix_d, w_out.astype(BF16), post_g, x, t["out_tm"])


def kernel(x, l0_pre_g, l0_w_in, l0_rpb, l0_lam_q1, l0_lam_k1, l0_lam_q2, l0_lam_k2, l0_subln_g,
           l0_w_out, l0_post_g, l1_pre_g, l1_w_in, l1_sink, l1_w_out, l1_post_g):
    b, s, d = x.shape
    assert b == 1 and d == D_MODEL
    y = x.reshape(s, d)
    y = _even_layer(y, l0_pre_g, l0_w_in, l0_rpb, l0_lam_q1, l0_lam_k1, l0_lam_q2, l0_lam_k2,
                    l0_subln_g, l0_w_out, l0_post_g)
    y = _odd_layer(y, l1_pre_g, l1_w_in, l1_sink, l1_w_out, l1_post_g)
    return y.reshape(b, s, d)
```

```python
import functools
import math

import numpy as np
import jax
import jax.numpy as jnp
from jax import lax
from jax.experimental import pallas as pl
from jax.experimental.pallas import tpu as pltpu

F32 = jnp.float32
BF16 = jnp.bfloat16

D_MODEL = 2048
HEAD_DIM = 128
N_HALF_HEADS = 8
SEC = N_HALF_HEADS * HEAD_DIM
GRID_W = 64
ROW_WIN = 8
COL_WIN = 16
DIFF_QK_DIM = 64
LAMBDA_INIT = 0.2
C_PATTERNS = ((128, 1), (512, 4), (2048, 16))
D_KV_HEADS = 2
D_HALF_WINDOW = 128
ROPE_THETA = 500000.0
ROPE_FRACTION = 4
RMS_EPS = 1e-6
NEG_INF = -1e30
LOG2_E = math.log2(math.e)

LANES = 128
MXU_N = 256
VMEM_LIMIT_BYTES = 56 * 1024 * 1024

PLAIN, ROPE = 0, 1
CHUNKS = SEC // LANES
ALL_PLAIN = (PLAIN,) * CHUNKS
ALL_ROPE = (ROPE,) * CHUNKS
NORM_ROWS = 256
MAIN_ROW_SPLIT = 4
DIFF_ONES_ROWS = 16
DIFF_UNROLL = 4
FOLD_SECTIONS = 3

NT_DIMS = (((1,), (1,)), ((), ()))


def _silu(g):
    return g / (1.0 + jnp.exp(-g))


SEC_PARAMS = 2 * CHUNKS + 1
FOLD, VT = "fold", "vt"


def _extra_sections(kind):
    return FOLD_SECTIONS if kind == FOLD else 1


def _inproj_kernel(n_main, extras, rope_half, par_ref, x_ref, g_ref, w_ref, cos_ref, sa_ref,
                   sb_ref, *rest):
    n_extra = len(extras)
    o_ref = rest[0]
    extra_refs = rest[1:1 + n_extra]
    h_ref = rest[1 + n_extra]
    acc_ref = rest[2 + n_extra] if any(kind == FOLD for kind, _ in extras) else None
    j = pl.program_id(1)
    tm = x_ref.shape[0]

    @pl.when(j == 0)
    def _():
        rc = min(NORM_ROWS, tm)
        for r in range(tm // rc):
            x = x_ref[r * rc:(r + 1) * rc, :]
            ms = jnp.mean(x * x, axis=-1, keepdims=True)
            h_ref[r * rc:(r + 1) * rc, :] = (x * lax.rsqrt(ms + RMS_EPS) * g_ref[...]).astype(BF16)

    base = j * SEC_PARAMS
    scale = par_ref[base + 2 * CHUNKS]

    def epilogue(a, c, rows=slice(None)):
        on = par_ref[base + c]
        off = par_ref[base + CHUNKS + c]
        a = (a * (cos_ref[rows, :] * on + off)
             + pltpu.roll(a, LANES - rope_half, 1) * (sa_ref[rows, :] * on)
             + pltpu.roll(a, rope_half, 1) * (sb_ref[rows, :] * on))
        return a * scale

    def section(dest):
        kind, dil = (None, None) if dest < 0 else extras[dest]
        if kind == VT:
            extra_refs[dest][:, HEAD_DIM:, :] = jnp.ones(
                (N_HALF_HEADS, DIFF_ONES_ROWS, tm), BF16)
        for nt in range(SEC // MXU_N):
            w_tile = w_ref[:, nt * MXU_N:(nt + 1) * MXU_N]
            chunks = range(nt * MXU_N // LANES, (nt + 1) * MXU_N // LANES)
            if kind == VT:
                acc = jnp.dot(h_ref[...], w_tile, preferred_element_type=F32)
                for c in chunks:
                    a = acc[:, (c * LANES) % MXU_N:(c * LANES) % MXU_N + LANES]
                    extra_refs[dest][c, 0:HEAD_DIM, :] = a.T.astype(BF16)
                continue
            nrow = tm // MAIN_ROW_SPLIT
            for rb in range(MAIN_ROW_SPLIT):
                rows = slice(rb * nrow, (rb + 1) * nrow)
                acc = jnp.dot(h_ref[rows, :], w_tile, preferred_element_type=F32)
                for c in chunks:
                    lanes = slice(c * LANES, (c + 1) * LANES)
                    a = epilogue(acc[:, (c * LANES) % MXU_N:(c * LANES) % MXU_N + LANES], c, rows)
                    if kind is None:
                        o_ref[rows, lanes] = a.astype(BF16)
                        continue
                    acc_ref[c, rows, :] = a
                    for r in range(dil):
                        src = pl.ds(rb * nrow + r, nrow // dil, stride=dil)
                        dst = slice(rb * nrow // dil, (rb + 1) * nrow // dil)
                        extra_refs[dest][r, dst, lanes] = acc_ref[c, src, :].astype(BF16)

    if n_extra == 0:
        section(-1)
    else:
        pl.when(j < n_main)(functools.partial(section, -1))
        first = n_main
        for e, (kind, _) in enumerate(extras):
            n = _extra_sections(kind)
            pl.when((j >= first) & (j < first + n))(functools.partial(section, e))
            first += n


def _inproj(x, pre_g, w, tables, rope_half, sections, n_main, extras, tm):
    s = x.shape[0]
    nsec = len(sections)
    assert nsec == n_main + sum(_extra_sections(kind) for kind, _ in extras)
    params = []
    for modes, scale in sections:
        params += [float(m == ROPE) for m in modes] + [float(m != ROPE) for m in modes] + [scale]
    params = jnp.asarray(np.asarray(params, np.float32))
    tab_spec = pl.BlockSpec((tm, LANES), lambda i, j, p: (i, 0))
    out_specs = [pl.BlockSpec((None, tm, SEC), lambda i, j, p: (jnp.minimum(j, n_main - 1), i, 0))]
    out_shape = [jax.ShapeDtypeStruct((n_main, s, SEC), BF16)]
    scratch = [pltpu.VMEM((tm, D_MODEL), BF16)]
    first = n_main
    for kind, dil in extras:
        if kind == FOLD:
            out_specs.append(pl.BlockSpec(
                (None, dil, tm // dil, SEC),
                lambda i, j, p, first=first: (jnp.clip(j - first, 0, FOLD_SECTIONS - 1), 0, i, 0)))
            out_shape.append(jax.ShapeDtypeStruct((FOLD_SECTIONS, dil, s // dil, SEC), BF16))
        else:
            rows = HEAD_DIM + DIFF_ONES_ROWS
            out_specs.append(pl.BlockSpec((N_HALF_HEADS, rows, tm), lambda i, j, p: (0, 0, i)))
            out_shape.append(jax.ShapeDtypeStruct((N_HALF_HEADS, rows, s), BF16))
        first += _extra_sections(kind)
    if any(kind == FOLD for kind, _ in extras):
        scratch.append(pltpu.VMEM((CHUNKS, tm, LANES), F32))
    grid_spec = pltpu.PrefetchScalarGridSpec(
        num_scalar_prefetch=1,
        grid=(s // tm, nsec),
        in_specs=[
            pl.BlockSpec((tm, D_MODEL), lambda i, j, p: (i, 0)),
            pl.BlockSpec((1, D_MODEL), lambda i, j, p: (0, 0)),
            pl.BlockSpec((D_MODEL, SEC), lambda i, j, p: (0, j)),
            tab_spec, tab_spec, tab_spec,
        ],
        out_specs=out_specs,
        scratch_shapes=scratch,
    )
    return pl.pallas_call(
        functools.partial(_inproj_kernel, n_main, tuple(extras), rope_half),
        grid_spec=grid_spec,
        out_shape=out_shape,
        compiler_params=pltpu.CompilerParams(
            dimension_semantics=("arbitrary", "arbitrary"),
            vmem_limit_bytes=VMEM_LIMIT_BYTES),
        name="inproj",
    )(params, x, pre_g.reshape(1, D_MODEL), w, *tables)


def _rope_tables(s, comp_dim):
    rd = comp_dim // ROPE_FRACTION
    half = rd // 2
    inv = ROPE_THETA ** (-jnp.arange(half, dtype=F32) / half)
    ang = jnp.arange(s).astype(F32)[:, None] * inv[None, :]
    cos, sin = jnp.cos(ang), jnp.sin(ang)
    ones = jnp.ones((s, comp_dim - rd), F32)
    zeros_h = jnp.zeros((s, half), F32)
    zeros_r = jnp.zeros((s, comp_dim - rd), F32)
    reps = LANES // comp_dim
    cos_t = jnp.tile(jnp.concatenate([cos, cos, ones], axis=1), (1, reps))
    sa_t = jnp.tile(jnp.concatenate([-sin, zeros_h, zeros_r], axis=1), (1, reps))
    sb_t = jnp.tile(jnp.concatenate([zeros_h, sin, zeros_r], axis=1), (1, reps))
    return (cos_t, sa_t, sb_t), half


NA_QROWS = 4
NA_KROWS = 12
NA_TQ = NA_QROWS * GRID_W
NA_KBLK = 3


def _natten_kernel(q_ref, k0_ref, k1_ref, k2_ref, v0_ref, v1_ref, v2_ref, g_ref, bias_ref,
                   o_ref):
    k_refs = (k0_ref, k1_ref, k2_ref)
    v_refs = (v0_ref, v1_ref, v2_ref)
    heads = range(N_HALF_HEADS)
    blocks = range(NA_KBLK)
    scores = []
    for h in heads:
        cols = slice(h * HEAD_DIM, (h + 1) * HEAD_DIM)
        q = q_ref[:, cols]
        scores.append([
            lax.dot_general(q, k_refs[t][:, cols], NT_DIMS, preferred_element_type=F32)
            + bias_ref[h, :, t * NA_TQ:(t + 1) * NA_TQ] for t in blocks])
    probs = []
    for h in heads:
        m = functools.reduce(jnp.maximum,
                             [jnp.max(st, axis=-1, keepdims=True) for st in scores[h]])
        probs.append([jnp.exp2(st - m).astype(BF16) for st in scores[h]])
    ones = jnp.ones((NA_TQ, HEAD_DIM), BF16)
    for h in heads:
        cols = slice(h * HEAD_DIM, (h + 1) * HEAD_DIM)
        acc = functools.reduce(jnp.add, [
            jnp.dot(probs[h][t], jnp.concatenate([v_refs[t][:, cols], ones], axis=1),
                    preferred_element_type=F32) for t in blocks])
        o = acc[:, 0:HEAD_DIM] / acc[:, HEAD_DIM:2 * HEAD_DIM]
        o_ref[:, cols] = (o * _silu(g_ref[:, cols].astype(F32))).astype(BF16)


def _natten_bias_table(rpb, rows):
    n_heads, n_dr, n_dc = rpb.shape
    w = GRID_W
    wr = min(ROW_WIN, rows)
    nblk = rows // NA_QROWS
    row_sel = np.zeros((3, NA_QROWS, NA_KROWS, n_dr), np.float32)
    for t, blk in enumerate((0, 1, nblk - 1)):
        r0 = blk * NA_QROWS
        ws = min(max(blk - 1, 0), nblk - NA_KBLK) * NA_QROWS
        for i in range(NA_QROWS):
            r = r0 + i
            rs = min(max(r - ROW_WIN // 2, 0), rows - wr)
            for jj in range(NA_KROWS):
                kr = ws + jj
                if rs <= kr < rs + wr:
                    row_sel[t, i, jj, kr - r + ROW_WIN - 1] = 1.0
    col_sel = np.zeros((n_dc, w, w), np.float32)
    for qc in range(w):
        cs = min(max(qc - COL_WIN // 2, 0), w - COL_WIN)
        for kc in range(cs, cs + COL_WIN):
            col_sel[kc - qc + COL_WIN - 1, qc, kc] = 1.0
    valid = (row_sel.sum(-1) > 0)[:, None, :, None, :, None] & (col_sel.sum(0) > 0)[None, None, None, :, None, :]
    exact = lax.Precision.HIGHEST
    by_row = jnp.einsum("hdc,tijd->htijc", rpb, jnp.asarray(row_sel), precision=exact)
    tab = jnp.einsum("htijc,cqk->thiqjk", by_row, jnp.asarray(col_sel), precision=exact)
    tab = jnp.where(jnp.asarray(valid), tab, NEG_INF)
    return tab.reshape(3, n_heads, NA_TQ, NA_KBLK * NA_TQ)


def _natten(proj, rpb):
    s = proj.shape[1]
    rows = s // GRID_W
    nblk = rows // NA_QROWS
    bias = _natten_bias_table(rpb.astype(F32) * LOG2_E, rows)

    def kv_spec(sec, t):
        return pl.BlockSpec(
            (None, NA_TQ, SEC),
            lambda b: (sec, jnp.clip(b - 1, 0, nblk - NA_KBLK) + t, 0))

    def bias_idx(b):
        return (jnp.where(b == 0, 0, jnp.where(b == nblk - 1, 2, 1)), 0, 0, 0)

    return pl.pallas_call(
        _natten_kernel,
        grid=(nblk,),
        in_specs=[
            pl.BlockSpec((None, NA_TQ, SEC), lambda b: (0, b, 0)),
            kv_spec(1, 0), kv_spec(1, 1), kv_spec(1, 2),
            kv_spec(2, 0), kv_spec(2, 1), kv_spec(2, 2),
            pl.BlockSpec((None, NA_TQ, SEC), lambda b: (3, b, 0)),
            pl.BlockSpec((None, N_HALF_HEADS, NA_TQ, NA_KBLK * NA_TQ), bias_idx),
        ],
        out_specs=pl.BlockSpec((NA_TQ, SEC), lambda b: (b, 0)),
        out_shape=jax.ShapeDtypeStruct((s, SEC), BF16),
        compiler_params=pltpu.CompilerParams(
            dimension_semantics=("arbitrary",), vmem_limit_bytes=VMEM_LIMIT_BYTES),
        name="natten",
    )(proj, proj, proj, proj, proj, proj, proj, proj, bias)


def _diff_kernel(tq, tk, q_ref, qn_ref, k_ref, vt_ref, g_ref, lam_ref, subg_ref, o_ref,
                 qbd_ref, s_ref, m_ref, acc_ref):
    s = k_ref.shape[0]
    n_chunks = s // tk
    n_iters = n_chunks // DIFF_UNROLL
    qi = pl.program_id(1)
    cur = lax.rem(qi, 2)
    nxt = 1 - cur

    def build_qbd(slot, q):
        first = lax.broadcasted_iota(jnp.int32, (tq, HEAD_DIM), 1) < DIFF_QK_DIM
        zero = jnp.zeros_like(q)
        qbd_ref[slot, 0:tq, :] = jnp.where(first, q, zero)
        qbd_ref[slot, tq:2 * tq, :] = jnp.where(first, zero, q)

    def scores(chunk, slot):
        start = pl.multiple_of(chunk * tk, tk)
        k = k_ref[pl.ds(start, tk), :]
        return lax.dot_general(k, qbd_ref[slot], NT_DIMS, preferred_element_type=F32)

    @pl.when(qi == 0)
    def _():
        build_qbd(0, q_ref[...])
        s_ref[0] = scores(0, 0)

    build_qbd(nxt, qn_ref[...])
    m_ref[...] = jnp.full(m_ref.shape, NEG_INF, F32)
    acc_ref[...] = jnp.zeros(acc_ref.shape, F32)

    def accumulate(slot, chunk):
        st = s_ref[slot]
        m_prev = m_ref[...]
        m_new = jnp.maximum(m_prev, jnp.max(st, axis=0, keepdims=True))
        alpha = jnp.exp2(m_prev - m_new)
        p = jnp.exp2(st - m_new).astype(BF16)
        start = pl.multiple_of(chunk * tk, tk)
        vt = vt_ref[:, pl.ds(start, tk)]
        acc_ref[...] = alpha * acc_ref[...] + jnp.dot(vt, p, preferred_element_type=F32)
        m_ref[...] = m_new

    def body(it, carry):
        c0 = DIFF_UNROLL * it
        last = it == n_iters - 1
        for u in range(DIFF_UNROLL):
            if u < DIFF_UNROLL - 1:
                s_ref[(u + 1) % 2] = scores(c0 + u + 1, cur)
            else:
                s_ref[(u + 1) % 2] = scores(jnp.where(last, 0, c0 + u + 1),
                                            jnp.where(last, nxt, cur))
            accumulate(u % 2, c0 + u)
        return carry

    lax.fori_loop(0, n_iters, body, 0)

    lam_v = lam_ref[...]
    lam = (jnp.exp(jnp.sum(lam_v[0:1] * lam_v[1:2], axis=-1, keepdims=True))
           - jnp.exp(jnp.sum(lam_v[2:3] * lam_v[3:4], axis=-1, keepdims=True))
           + LAMBDA_INIT)
    on = acc_ref[0:HEAD_DIM, :] / acc_ref[HEAD_DIM:HEAD_DIM + 1, :]
    ot = on[:, 0:tq] - lam * on[:, tq:2 * tq]
    ms = jnp.mean(ot * ot, axis=0, keepdims=True)
    ot = ot * lax.rsqrt(ms + RMS_EPS) * subg_ref[...] * (1.0 - LAMBDA_INIT)
    o = ot.T
    o_ref[...] = (o * _silu(g_ref[...].astype(F32))).astype(BF16)


def _diff_attention(proj, vt, lam_vecs, subln_g, tq, tk):
    s = proj.shape[1]
    vt_rows = vt.shape[1]
    assert (s // tk) % DIFF_UNROLL == 0 and DIFF_UNROLL % 2 == 0
    n_q = s // tq
    return pl.pallas_call(
        functools.partial(_diff_kernel, tq, tk),
        grid=(N_HALF_HEADS, n_q),
        in_specs=[
            pl.BlockSpec((None, tq, HEAD_DIM), lambda h, i: (4, i, h)),
            pl.BlockSpec((None, tq, HEAD_DIM), lambda h, i: (4, jnp.minimum(i + 1, n_q - 1), h)),
            pl.BlockSpec((None, s, HEAD_DIM), lambda h, i: (5, 0, h)),
            pl.BlockSpec((None, vt_rows, s), lambda h, i: (h, 0, 0)),
            pl.BlockSpec((None, tq, HEAD_DIM), lambda h, i: (6, i, h)),
            pl.BlockSpec((4, DIFF_QK_DIM), lambda h, i: (0, 0)),
            pl.BlockSpec((HEAD_DIM, 1), lambda h, i: (0, 0)),
        ],
        out_specs=pl.BlockSpec((tq, HEAD_DIM), lambda h, i: (i, h)),
        out_shape=jax.ShapeDtypeStruct((s, SEC), BF16),
        scratch_shapes=[
            pltpu.VMEM((2, 2 * tq, HEAD_DIM), BF16),
            pltpu.VMEM((2, tk, 2 * tq), F32),
            pltpu.VMEM((1, 2 * tq), F32),
            pltpu.VMEM((vt_rows, 2 * tq), F32),
        ],
        compiler_params=pltpu.CompilerParams(
            dimension_semantics=("arbitrary", "arbitrary"),
            vmem_limit_bytes=VMEM_LIMIT_BYTES),
        name="diff_attn",
    )(proj, proj, proj, vt, proj, lam_vecs, subln_g.reshape(HEAD_DIM, 1))


def _banded_kernel(tq, half, kcols, vcols, with_sink, *refs):
    if with_sink:
        (sink_ref, q_ref, km_ref, kp_ref, kn_ref, vm_ref, vp_ref, vn_ref, g_ref, o_ref) = refs
    else:
        (q_ref, km_ref, kp_ref, kn_ref, vm_ref, vp_ref, vn_ref, o_ref, lse_ref) = refs
    lb = pl.program_id(1)
    nblk = pl.num_programs(1)

    qi = lax.broadcasted_iota(jnp.int32, (tq, tq), 0)
    kj = lax.broadcasted_iota(jnp.int32, (tq, tq), 1)
    ok_main = jnp.abs(qi - kj) <= half
    qh = lax.broadcasted_iota(jnp.int32, (tq, 2 * half), 0)
    jh = lax.broadcasted_iota(jnp.int32, (tq, 2 * half), 1)
    prev_lo = jnp.where(lb > 0, 0, half)
    next_hi = jnp.where(lb < nblk - 1, 2 * half, half)
    ok_prev = (jh >= prev_lo) & (jh < half) & (qh <= jh)
    ok_next = (jh >= half) & (jh < next_hi) & (tq + (jh - half) - qh <= half)
    ok_halo = ok_prev | ok_next

    if not with_sink:
        lse_ref[...] = jnp.zeros(lse_ref.shape, F32)

    heads = range(N_HALF_HEADS)
    s_main, s_halo = [], []
    for h in heads:
        kc = slice(kcols[h], kcols[h] + HEAD_DIM)
        q = q_ref[:, h * HEAD_DIM:(h + 1) * HEAD_DIM]
        k_halo = jnp.concatenate([kp_ref[:, kc], kn_ref[:, kc]], axis=0)
        s_m = lax.dot_general(q, km_ref[:, kc], NT_DIMS, preferred_element_type=F32)
        s_h = lax.dot_general(q, k_halo, NT_DIMS, preferred_element_type=F32)
        s_main.append(jnp.where(ok_main, s_m, NEG_INF))
        s_halo.append(jnp.where(ok_halo, s_h, NEG_INF))
    maxes, p_main, p_halo = [], [], []
    for h in heads:
        m = jnp.maximum(jnp.max(s_main[h], axis=-1, keepdims=True),
                        jnp.max(s_halo[h], axis=-1, keepdims=True))
        maxes.append(m)
        p_main.append(jnp.exp2(s_main[h] - m).astype(BF16))
        p_halo.append(jnp.exp2(s_halo[h] - m).astype(BF16))
    ones_main = jnp.ones((tq, HEAD_DIM), BF16)
    ones_halo = jnp.ones((2 * half, HEAD_DIM), BF16)
    for h in heads:
        cols = slice(h * HEAD_DIM, (h + 1) * HEAD_DIM)
        vc = slice(vcols[h], vcols[h] + HEAD_DIM)
        v_main = jnp.concatenate([vm_ref[:, vc], ones_main], axis=1)
        v_halo = jnp.concatenate(
            [jnp.concatenate([vp_ref[:, vc], vn_ref[:, vc]], axis=0), ones_halo], axis=1)
        acc = (jnp.dot(p_main[h], v_main, preferred_element_type=F32)
               + jnp.dot(p_halo[h], v_halo, preferred_element_type=F32))
        num, den = acc[:, 0:HEAD_DIM], acc[:, HEAD_DIM:2 * HEAD_DIM]
        if with_sink:
            o = num / (den + jnp.exp2(sink_ref[h] - maxes[h]))
            o_ref[:, cols] = (o * _silu(g_ref[:, cols].astype(F32))).astype(BF16)
        else:
            o_ref[:, cols] = num / den
            lse_ref[:, h:h + 1] = maxes[h] + jnp.log2(den[:, 0:1])


def _banded_specs(tq, half, q_sec, k_sec, v_sec, nblk):
    r = tq // half

    def main(sec):
        return pl.BlockSpec((None, None, tq, SEC), lambda c, lb, *_: (sec, c, lb, 0))

    def prev(sec):
        return pl.BlockSpec((None, None, half, SEC),
                            lambda c, lb, *_: (sec, c, jnp.maximum(lb * r - 1, 0), 0))

    def nxt(sec):
        return pl.BlockSpec((None, None, half, SEC),
                            lambda c, lb, *_: (sec, c, jnp.minimum((lb + 1) * r, nblk * r - 1), 0))

    return [main(q_sec), main(k_sec), prev(k_sec), nxt(k_sec),
            main(v_sec), prev(v_sec), nxt(v_sec)]


def _dilated_group(qkv, half, tq):
    _, dil, length, _ = qkv.shape
    tq = min(tq, length)
    nblk = length // tq
    cols = tuple(h * HEAD_DIM for h in range(N_HALF_HEADS))
    return pl.pallas_call(
        functools.partial(_banded_kernel, tq, half, cols, cols, False),
        grid=(dil, nblk),
        in_specs=_banded_specs(tq, half, 0, 1, 2, nblk),
        out_specs=[pl.BlockSpec((None, tq, SEC), lambda c, lb: (c, lb, 0)),
                   pl.BlockSpec((None, tq, LANES), lambda c, lb: (c, lb, 0))],
        out_shape=[jax.ShapeDtypeStruct((dil, length, SEC), F32),
                   jax.ShapeDtypeStruct((dil, length, LANES), F32)],
        compiler_params=pltpu.CompilerParams(
            dimension_semantics=("arbitrary", "arbitrary"),
            vmem_limit_bytes=VMEM_LIMIT_BYTES),
        name=f"dilated_x{dil}",
    )(*([qkv] * 7))


def _windowed_sink(proj, sink, tq):
    s = proj.shape[2]
    half = D_HALF_WINDOW
    nblk = s // tq
    grp = N_HALF_HEADS // D_KV_HEADS
    kcols = tuple((h // grp) * HEAD_DIM for h in range(N_HALF_HEADS))
    vcols = tuple(D_KV_HEADS * HEAD_DIM + (h // grp) * HEAD_DIM for h in range(N_HALF_HEADS))
    specs = _banded_specs(tq, half, 4, 5, 5, nblk)
    specs.append(pl.BlockSpec((None, None, tq, SEC), lambda c, lb, *_: (6, c, lb, 0)))
    grid_spec = pltpu.PrefetchScalarGridSpec(
        num_scalar_prefetch=1,
        grid=(1, nblk),
        in_specs=specs,
        out_specs=pl.BlockSpec((tq, SEC), lambda c, lb, *_: (lb, 0)),
    )
    return pl.pallas_call(
        functools.partial(_banded_kernel, tq, half, kcols, vcols, True),
        grid_spec=grid_spec,
        out_shape=jax.ShapeDtypeStruct((s, SEC), BF16),
        compiler_params=pltpu.CompilerParams(
            dimension_semantics=("arbitrary", "arbitrary"),
            vmem_limit_bytes=VMEM_LIMIT_BYTES),
        name="windowed_sink",
    )(sink.astype(F32) * LOG2_E, *([proj] * 8))


def _merge_into(out_ref, dils, o_refs, l_refs, g_ref, scratch):
    tm = out_ref.shape[0]
    head_out, l_tok = [], []
    for gi, dil in enumerate(dils):
        if dil == 1:
            head_out.append(lambda h, r=o_refs[gi]: r[0, :, h * HEAD_DIM:(h + 1) * HEAD_DIM])
            l_tok.append(l_refs[gi].at[0])
            continue
        ou_ref, lu_ref = scratch[2 * (gi - 1)], scratch[2 * (gi - 1) + 1]
        for c in range(dil):
            rows = pl.ds(c, tm // dil, stride=dil)
            for h in range(N_HALF_HEADS):
                ou_ref[h, rows, :] = o_refs[gi][c, :, h * HEAD_DIM:(h + 1) * HEAD_DIM]
            lu_ref[rows, :] = l_refs[gi][c]
        head_out.append(lambda h, r=ou_ref: r[h])
        l_tok.append(lu_ref)
    for h in range(N_HALF_HEADS):
        cols = slice(h * HEAD_DIM, (h + 1) * HEAD_DIM)
        ls = [r[:, h:h + 1] for r in l_tok]
        m = jnp.maximum(jnp.maximum(ls[0], ls[1]), ls[2])
        es = [jnp.exp2(x - m) for x in ls]
        den = es[0] + es[1] + es[2]
        o = ((es[0] / den) * head_out[0](h) + (es[1] / den) * head_out[1](h)
             + (es[2] / den) * head_out[2](h))
        out_ref[:, cols] = (o * _silu(g_ref[:, cols].astype(F32))).astype(BF16)


def _project_out(a, b_ref, w_ref, g_ref, x_ref, o_ref):
    y = (jnp.dot(a, w_ref[0:SEC, :], preferred_element_type=F32)
         + jnp.dot(b_ref[...], w_ref[SEC:2 * SEC, :], preferred_element_type=F32))
    ms = jnp.mean(y * y, axis=-1, keepdims=True)
    o_ref[...] = x_ref[...] + y * lax.rsqrt(ms + RMS_EPS) * g_ref[...]


def _outproj_kernel(a_ref, b_ref, w_ref, g_ref, x_ref, o_ref):
    _project_out(a_ref[...], b_ref, w_ref, g_ref, x_ref, o_ref)


def _outproj_merge_kernel(dils, *refs):
    n = len(dils)
    o_refs, l_refs = refs[0:n], refs[n:2 * n]
    gate_ref, b_ref, w_ref, g_ref, x_ref, o_ref, mix_ref = refs[2 * n:2 * n + 7]
    _merge_into(mix_ref, dils, o_refs, l_refs, gate_ref, refs[2 * n + 7:])
    _project_out(mix_ref[...], b_ref, w_ref, g_ref, x_ref, o_ref)


def _outproj(mix_a, mix_b, w_out, post_g, x, tm):
    s = x.shape[0]
    half_spec = pl.BlockSpec((tm, SEC), lambda i: (i, 0))
    tail_specs = [half_spec,
                  pl.BlockSpec((2 * SEC, D_MODEL), lambda i: (0, 0), pipeline_mode=pl.Buffered(1)),
                  pl.BlockSpec((1, D_MODEL), lambda i: (0, 0)),
                  pl.BlockSpec((tm, D_MODEL), lambda i: (i, 0))]
    tail_args = (mix_b, w_out, post_g.reshape(1, D_MODEL), x)
    if isinstance(mix_a, tuple):
        outs, lses, proj, gate_sec = mix_a
        dils = tuple(o.shape[0] for o in outs)
        assert dils[0] == 1
        in_specs = [pl.BlockSpec((d, tm // d, SEC), lambda i: (0, i, 0)) for d in dils]
        in_specs += [pl.BlockSpec((d, tm // d, LANES), lambda i: (0, i, 0)) for d in dils]
        in_specs.append(pl.BlockSpec((None, tm, SEC), lambda i: (gate_sec, i, 0)))
        scratch = [pltpu.VMEM((tm, SEC), BF16)]
        for d in dils[1:]:
            scratch += [pltpu.VMEM((N_HALF_HEADS, tm, HEAD_DIM), F32), pltpu.VMEM((tm, LANES), F32)]
        body = functools.partial(_outproj_merge_kernel, dils)
        args = (*outs, *lses, proj)
    else:
        in_specs, scratch, body, args = [half_spec], [], _outproj_kernel, (mix_a,)
    return pl.pallas_call(
        body,
        grid=(s // tm,),
        in_specs=in_specs + tail_specs,
        out_specs=pl.BlockSpec((tm, D_MODEL), lambda i: (i, 0)),
        out_shape=jax.ShapeDtypeStruct((s, D_MODEL), F32),
        scratch_shapes=scratch,
        compiler_params=pltpu.CompilerParams(
            dimension_semantics=("arbitrary",), vmem_limit_bytes=VMEM_LIMIT_BYTES),
        name="outproj",
    )(*args, *tail_args)


def _tiles(s):
    return dict(proj_tm=min(1024, s), out_tm=min(512, s), diff_tq=min(512, s),
                diff_tk=min(1024, s // 4), band_tq=min(256, s))


def _even_layer(x, pre_g, w_in, rpb, lam_q1, lam_k1, lam_q2, lam_k2, subln_g, w_out, post_g):
    s = x.shape[0]
    t = _tiles(s)
    tables, rope_half = _rope_tables(s, DIFF_QK_DIM)
    sections = [
        (ALL_PLAIN, HEAD_DIM ** -0.5 * LOG2_E), (ALL_PLAIN, 1.0), (ALL_PLAIN, 1.0), (ALL_PLAIN, 1.0),
        (ALL_ROPE, DIFF_QK_DIM ** -0.5 * LOG2_E), (ALL_ROPE, 1.0), (ALL_PLAIN, 1.0),
        (ALL_PLAIN, 1.0),
    ]
    n_main = len(sections) - 1
    w = jnp.concatenate([w_in[:, :6 * SEC], w_in[:, 7 * SEC:], w_in[:, 6 * SEC:7 * SEC]],
                        axis=1).astype(BF16)
    proj, vt = _inproj(x, pre_g, w, tables, rope_half, sections, n_main, ((VT, None),),
                       t["proj_tm"])
    mix_a = _natten(proj, rpb)
    lam_vecs = jnp.stack([lam_q1, lam_k1, lam_q2, lam_k2]).astype(F32)
    mix_b = _diff_attention(proj, vt, lam_vecs, subln_g, t["diff_tq"], t["diff_tk"])
    return _outproj(mix_a, mix_b, w_out.astype(BF16), post_g, x, t["out_tm"])


def _odd_layer(x, pre_g, w_in, sink, w_out, post_g):
    s = x.shape[0]
    t = _tiles(s)
    tables, rope_half = _rope_tables(s, HEAD_DIM)
    scale = HEAD_DIM ** -0.5 * LOG2_E
    n_groups = len(C_PATTERNS)
    kv_w = D_KV_HEADS * HEAD_DIM
    qc, kc, vc = (w_in[:, i * n_groups * SEC:(i + 1) * n_groups * SEC] for i in range(3))
    rest = w_in[:, 3 * n_groups * SEC:]
    gc, qd, kd, vd, gd = (rest[:, :SEC], rest[:, SEC:2 * SEC], rest[:, 2 * SEC:2 * SEC + kv_w],
                          rest[:, 2 * SEC + kv_w:2 * SEC + 2 * kv_w], rest[:, 2 * SEC + 2 * kv_w:])

    def group(w3, gi):
        return w3[:, gi * SEC:(gi + 1) * SEC]

    kvd = jnp.concatenate([kd, vd, jnp.zeros((D_MODEL, SEC - 2 * kv_w), w_in.dtype)], axis=1)
    blocks = [group(qc, 0), group(kc, 0), group(vc, 0), gc, qd, kvd, gd]
    kv_modes = (ROPE,) * D_KV_HEADS + (PLAIN,) * (CHUNKS - D_KV_HEADS)
    sections = [(ALL_ROPE, scale), (ALL_ROPE, 1.0), (ALL_PLAIN, 1.0), (ALL_PLAIN, 1.0),
                (ALL_ROPE, scale), (kv_modes, 1.0), (ALL_PLAIN, 1.0)]
    n_main = len(sections)
    extras = tuple((FOLD, dil) for _, dil in C_PATTERNS[1:])
    for gi in range(1, n_groups):
        blocks += [group(qc, gi), group(kc, gi), group(vc, gi)]
        sections += [(ALL_ROPE, scale), (ALL_ROPE, 1.0), (ALL_PLAIN, 1.0)]
    w = jnp.concatenate(blocks, axis=1).astype(BF16)
    proj, *folds = _inproj(x, pre_g, w, tables, rope_half, sections, n_main, extras, t["proj_tm"])
    proj4 = proj.reshape(n_main, 1, s, SEC)
    outs, lses = [], []
    for gi, (window, dil) in enumerate(C_PATTERNS):
        o, lse = _dilated_group(proj4 if gi == 0 else folds[gi - 1], (window // 2) // dil,
                                t["band_tq"])
        outs.append(o)
        lses.append(lse)
    mix_d = _windowed_sink(proj4, sink, t["band_tq"])
    return _outproj((outs, lses, proj, 3), mix_d, w_out.astype(BF16), post_g, x, t["out_tm"])


def kernel(x, l0_pre_g, l0_w_in, l0_rpb, l0_lam_q1, l0_lam_k1, l0_lam_q2, l0_lam_k2, l0_subln_g,
           l0_w_out, l0_post_g, l1_pre_g, l1_w_in, l1_sink, l1_w_out, l1_post_g):
    b, s, d = x.shape
    assert b == 1 and d == D_MODEL
    y = x.reshape(s, d)
    y = _even_layer(y, l0_pre_g, l0_w_in, l0_rpb, l0_lam_q1, l0_lam_k1, l0_lam_q2, l0_lam_k2,
                    l0_subln_g, l0_w_out, l0_post_g)
    y = _odd_layer(y, l1_pre_g, l1_w_in, l1_sink, l1_w_out, l1_post_g)
    return y.reshape(b, s, d)
```
